```python
import math
import jax, jax.numpy as jnp
from jax import lax
import numpy as np


D_MODEL = 2048
BATCH = 1
SEQ = 16384
DEPTH = 2

D_MIX = D_MODEL
C_POOL = D_MIX // 4
POOL_WINDOWS = (2, 4, 8, 16)
POOL_GROUP = C_POOL // len(POOL_WINDOWS)
C_CONV = D_MIX // 4
CONV_K = 31
FOX_HEADS = 4
FOX_HEAD_DIM = D_MIX // 4 // FOX_HEADS
FOX_DIM = FOX_HEADS * FOX_HEAD_DIM
C_SC = D_MIX - C_POOL - C_CONV - FOX_DIM
SC_K = 3
Q_BLOCK = 128
P_IN = C_POOL + 2 * C_CONV + 3 * FOX_DIM + FOX_HEADS + 3 * C_SC
N_MEM = 256
XA_HEADS = 4
XA_HEAD_DIM = 128
XA_DIM = XA_HEADS * XA_HEAD_DIM
D_FF = ((8 * D_MODEL // 3 + 255) // 256) * 256
EPS = 1e-6
FORGET_BIAS_INIT = 2.0

kernel_name = "hymba_style_pool_conformer_fox_shortconv_macaron"


def _split(z, sizes):
    offs = []
    acc = 0
    for s in sizes[:-1]:
        acc += s
        offs.append(acc)
    return jnp.split(z, offs, axis=-1)


def rmsnorm(x, g):
    xf = x.astype(jnp.float32)
    y = xf * lax.rsqrt(jnp.mean(xf * xf, axis=-1, keepdims=True) + EPS)
    return (y * g.astype(jnp.float32)).astype(x.dtype)


def layernorm(x, g, b):
    xf = x.astype(jnp.float32)
    mu = jnp.mean(xf, axis=-1, keepdims=True)
    xc = xf - mu
    var = jnp.mean(xc * xc, axis=-1, keepdims=True)
    y = xc * lax.rsqrt(var + EPS) * g.astype(jnp.float32) + b.astype(jnp.float32)
    return y.astype(x.dtype)


def swiglu_ffn(h, w_gate, w_up, w_down):
    return (jax.nn.silu(h @ w_gate) * (h @ w_up)) @ w_down


def causal_depthwise_conv(u, w):
    K, C = w.shape
    return lax.conv_general_dilated(
        u, w[:, None, :], window_strides=(1,), padding=((K - 1, 0),),
        dimension_numbers=('NWC', 'WIO', 'NWC'), feature_group_count=C)


def pool_mixer(u, pool_w, pool_scale):
    B, S, C = u.shape
    uf = u.astype(jnp.float32)
    cs0 = jnp.concatenate([jnp.zeros((B, 1, C), jnp.float32), jnp.cumsum(uf, axis=1)], axis=1)
    outs = []
    for g, w in enumerate(POOL_WINDOWS):
        sl = slice(g * POOL_GROUP, (g + 1) * POOL_GROUP)
        c0 = cs0[:, :, sl]
        start = jnp.concatenate([jnp.zeros((B, w - 1, POOL_GROUP), jnp.float32), c0[:, :S - w + 1]], axis=1)
        count = jnp.minimum(jnp.arange(1, S + 1, dtype=jnp.float32), float(w))[None, :, None]
        outs.append((c0[:, 1:] - start) / count - uf[:, :, sl])
    p = jnp.stack(outs, axis=2).astype(u.dtype)
    y = jnp.einsum('bsgc,gcd->bsgd', p, pool_w).reshape(B, S, C)
    return y * pool_scale


def conformer_conv(a, gate, dw_w, dw_b, ln_g, ln_b, pw):
    u = a * jax.nn.sigmoid(gate)
    u = causal_depthwise_conv(u, dw_w) + dw_b
    u = jax.nn.silu(layernorm(u, ln_g, ln_b))
    return u @ pw


def forgetting_attention(q, k, v, f_logit, f_bias):
    B, S, _ = q.shape
    q = q.reshape(B, S, FOX_HEADS, FOX_HEAD_DIM)
    k = k.reshape(B, S, FOX_HEADS, FOX_HEAD_DIM)
    v = v.reshape(B, S, FOX_HEADS, FOX_HEAD_DIM)
    log_f = jax.nn.log_sigmoid((f_logit + f_bias).astype(jnp.float32))
    c = jnp.transpose(jnp.cumsum(log_f, axis=1), (0, 2, 1))
    scale = FOX_HEAD_DIM ** -0.5
    kpos = jnp.arange(S)

    def one_block(i):
        start = i * Q_BLOCK
        qb = lax.dynamic_slice_in_dim(q, start, Q_BLOCK, axis=1)
        cb = lax.dynamic_slice_in_dim(c, start, Q_BLOCK, axis=2)
        s = jnp.einsum('bqhd,bkhd->bhqk', qb, k).astype(jnp.float32) * scale
        s = s + (cb[..., :, None] - c[..., None, :])
        qpos = start + jnp.arange(Q_BLOCK)
        s = jnp.where(kpos[None, :] <= qpos[:, None], s, -jnp.inf)
        p = jax.nn.softmax(s, axis=-1).astype(v.dtype)
        return jnp.einsum('bhqk,bkhd->bqhd', p, v)

    o = lax.map(one_block, jnp.arange(S // Q_BLOCK))
    return jnp.transpose(o, (1, 0, 2, 3, 4)).reshape(B, S, FOX_DIM)


def short_gated_conv(bg, cg, xin, w):
    return bg * causal_depthwise_conv(cg * xin, w)


def memory_cross_attention(h, mem_n, w_q, w_k, w_v, w_o):
    B, S, _ = h.shape
    M = mem_n.shape[1]
    q = (h @ w_q).reshape(B, S, XA_HEADS, XA_HEAD_DIM)
    k = (mem_n @ w_k).reshape(B, M, XA_HEADS, XA_HEAD_DIM)
    v = (mem_n @ w_v).reshape(B, M, XA_HEADS, XA_HEAD_DIM)
    s = jnp.einsum('bqhd,bkhd->bhqk', q, k).astype(jnp.float32) * (XA_HEAD_DIM ** -0.5)
    p = jax.nn.softmax(s, axis=-1).astype(v.dtype)
    o = jnp.einsum('bhqk,bkhd->bqhd', p, v).reshape(B, S, XA_DIM)
    return o @ w_o


def setup_inputs(seed: int = 0) -> dict:
    key = jax.random.key(seed)
    ks = iter(jax.random.split(key, 48))
    L = DEPTH

    def nrm(shape, scale):
        return jax.random.normal(next(ks), shape, jnp.float32) * scale

    def gain(shape):
        return 1.0 + nrm(shape, 0.02)

    return {
        "x": nrm((BATCH, SEQ, D_MODEL), 1.0),
        "mem": nrm((BATCH, N_MEM, D_MODEL), 1.0),
        "ffn1_norm": gain((L, D_MODEL)),
        "ffn1_w_gate": nrm((L, D_MODEL, D_FF), D_MODEL ** -0.5),
        "ffn1_w_up": nrm((L, D_MODEL, D_FF), D_MODEL ** -0.5),
        "ffn1_w_down": nrm((L, D_FF, D_MODEL), D_FF ** -0.5),
        "mix_norm": gain((L, D_MODEL)),
        "w_mix_in": nrm((L, D_MODEL, P_IN), D_MODEL ** -0.5),
        "pool_w": nrm((L, len(POOL_WINDOWS), POOL_GROUP, POOL_GROUP), POOL_GROUP ** -0.5),
        "pool_scale": gain((L, C_POOL)),
        "conv_dw_w": nrm((L, CONV_K, C_CONV), CONV_K ** -0.5),
        "conv_dw_b": nrm((L, C_CONV), 0.02),
        "conv_ln_g": gain((L, C_CONV)),
        "conv_ln_b": nrm((L, C_CONV), 0.02),
        "conv_pw": nrm((L, C_CONV, C_CONV), C_CONV ** -0.5),
        "fox_f_bias": FORGET_BIAS_INIT + nrm((L, FOX_HEADS), 0.5),
        "sc_w": nrm((L, SC_K, C_SC), SC_K ** -0.5),
        "grp_norm": gain((L, D_MIX)),
        "w_mix_out": nrm((L, D_MIX, D_MODEL), D_MIX ** -0.5),
        "xa_norm": gain((L, D_MODEL)),
        "mem_norm": gain((L, D_MODEL)),
        "w_xq": nrm((L, D_MODEL, XA_DIM), D_MODEL ** -0.5),
        "w_xk": nrm((L, D_MODEL, XA_DIM), D_MODEL ** -0.5),
        "w_xv": nrm((L, D_MODEL, XA_DIM), D_MODEL ** -0.5),
        "w_xo": nrm((L, XA_DIM, D_MODEL), XA_DIM ** -0.5),
        "ffn2_norm": gain((L, D_MODEL)),
        "ffn2_w_gate": nrm((L, D_MODEL, D_FF), D_MODEL ** -0.5),
        "ffn2_w_up": nrm((L, D_MODEL, D_FF), D_MODEL ** -0.5),
        "ffn2_w_down": nrm((L, D_FF, D_MODEL), D_FF ** -0.5),
        "final_norm": gain((D_MODEL,)),
    }


def reference(x, mem, ffn1_norm, ffn1_w_gate, ffn1_w_up, ffn1_w_down, mix_norm, w_mix_in,
              pool_w, pool_scale, conv_dw_w, conv_dw_b, conv_ln_g, conv_ln_b, conv_pw,
              fox_f_bias, sc_w, grp_norm, w_mix_out, xa_norm, mem_norm, w_xq, w_xk, w_xv, w_xo,
              ffn2_norm, ffn2_w_gate, ffn2_w_up, ffn2_w_down, final_norm):
    in_sizes = (C_POOL, C_CONV, C_CONV, FOX_DIM, FOX_DIM, FOX_DIM, FOX_HEADS, C_SC, C_SC, C_SC)
    out_sizes = (C_POOL, C_CONV, FOX_DIM, C_SC)
    for l in range(DEPTH):
        h = rmsnorm(x, ffn1_norm[l])
        x = x + 0.5 * swiglu_ffn(h, ffn1_w_gate[l], ffn1_w_up[l], ffn1_w_down[l])

        h = rmsnorm(x, mix_norm[l])
        z = h @ w_mix_in[l]
        (u_pool, glu_a, glu_g, q, k, v, f_logit, sc_b, sc_c, sc_x) = _split(z, in_sizes)
        y_a = pool_mixer(u_pool, pool_w[l], pool_scale[l])
        y_b = conformer_conv(glu_a, glu_g, conv_dw_w[l], conv_dw_b[l], conv_ln_g[l], conv_ln_b[l], conv_pw[l])
        y_c = forgetting_attention(q, k, v, f_logit, fox_f_bias[l])
        y_d = short_gated_conv(sc_b, sc_c, sc_x, sc_w[l])
        g_a, g_b, g_c, g_d = _split(grp_norm[l], out_sizes)
        y = jnp.concatenate([rmsnorm(y_a, g_a), rmsnorm(y_b, g_b), rmsnorm(y_c, g_c), rmsnorm(y_d, g_d)], axis=-1)
        x = x + y @ w_mix_out[l]

        h = rmsnorm(x, xa_norm[l])
        mem_n = rmsnorm(mem, mem_norm[l])
        x = x + memory_cross_attention(h, mem_n, w_xq[l], w_xk[l], w_xv[l], w_xo[l])

        h = rmsnorm(x, ffn2_norm[l])
        x = x + 0.5 * swiglu_ffn(h, ffn2_w_gate[l], ffn2_w_up[l], ffn2_w_down[l])
    return rmsnorm(x, final_norm)
```

```python
import functools

import jax
import jax.numpy as jnp
from jax import lax
from jax.experimental import pallas as pl
from jax.experimental.pallas import tpu as pltpu

F32 = jnp.float32
BF16 = jnp.bfloat16

D_MODEL = 2048
SEQ = 16384
DEPTH = 2
C_GRP = 512
POOL_WINDOWS = (2, 4, 8, 16)
POOL_GROUP = 128
CONV_K = 31
SC_K = 3
HEADS = 4
HEAD_DIM = 128
N_MEM = 256
D_FF = 5632
EPS = 1e-6
ATTN_SCALE = HEAD_DIM ** -0.5
LANES = 128
HALO = 32
NEG_BIG = -1e30
VMEM_LIMIT = 56 * 1024 * 1024

FFN_TM, FFN_TF = 512, 512
MIX_TM = 256
ATT_T = 512
OUT_TM = 256


def _rms(x, g):
    ms = jnp.mean(x * x, axis=-1, keepdims=True)
    return x * lax.rsqrt(ms + EPS) * g


def _dot(a, b):
    return jnp.dot(a, b, preferred_element_type=F32)


def _dot_nt(a, b):
    return lax.dot_general(a, b, (((1,), (1,)), ((), ())), preferred_element_type=F32)


def _const_spec(shape):
    return pl.BlockSpec(shape, lambda *_: (0,) * len(shape), pipeline_mode=pl.Buffered(1))


def _params(*sem):
    return pltpu.CompilerParams(dimension_semantics=sem, vmem_limit_bytes=VMEM_LIMIT)


def _ffn_kernel(*refs, final):
    if final:
        x_ref, g_ref, wg_ref, wu_ref, wd_ref, gf_ref, o_ref, h_ref = refs
    else:
        x_ref, g_ref, wg_ref, wu_ref, wd_ref, o_ref, h_ref = refs
    j = pl.program_id(1)

    @pl.when(j == 0)
    def _():
        h_ref[...] = _rms(x_ref[...], g_ref[...]).astype(BF16)
        o_ref[...] = jnp.zeros_like(o_ref)

    h = h_ref[...]
    gate = _dot(h, wg_ref[...])
    up = _dot(h, wu_ref[...])
    a = (gate * jax.nn.sigmoid(gate) * up).astype(BF16)
    o_ref[...] += _dot(a, wd_ref[...])

    @pl.when(j == pl.num_programs(1) - 1)
    def _():
        y = x_ref[...] + 0.5 * o_ref[...]
        if final:
            y = _rms(y, gf_ref[...])
        o_ref[...] = y


def _ffn(x, g, wg, wu, wd, gf=None):
    final = gf is not None
    tm, tf = FFN_TM, FFN_TF
    in_specs = [
        pl.BlockSpec((tm, D_MODEL), lambda i, j: (i, 0)),
        _const_spec((1, D_MODEL)),
        pl.BlockSpec((D_MODEL, tf), lambda i, j: (0, j)),
        pl.BlockSpec((D_MODEL, tf), lambda i, j: (0, j)),
        pl.BlockSpec((tf, D_MODEL), lambda i, j: (j, 0)),
    ]
    args = [x, g, wg, wu, wd]
    if final:
        in_specs.append(_const_spec((1, D_MODEL)))
        args.append(gf)
    return pl.pallas_call(
        functools.partial(_ffn_kernel, final=final),
        out_shape=jax.ShapeDtypeStruct((SEQ, D_MODEL), F32),
        grid=(SEQ // tm, D_FF // tf),
        in_specs=in_specs,
        out_specs=pl.BlockSpec((tm, D_MODEL), lambda i, j: (i, 0)),
        scratch_shapes=[pltpu.VMEM((tm, D_MODEL), BF16)],
        compiler_params=_params("arbitrary", "arbitrary"),
        name="ffn_final" if final else "ffn",
    )(*args)


def _mem_kv_kernel(mem_ref, g_ref, wk_ref, wv_ref, k_ref, v_ref):
    mn = _rms(mem_ref[...], g_ref[...]).astype(BF16)
    k_ref[...] = _dot(mn, wk_ref[...]).astype(BF16)
    v_ref[...] = _dot(mn, wv_ref[...]).astype(BF16)


def _mem_kv(mem, g, wk, wv):
    return pl.pallas_call(
        _mem_kv_kernel,
        out_shape=[jax.ShapeDtypeStruct((N_MEM, C_GRP), BF16)] * 2,
        compiler_params=pltpu.CompilerParams(vmem_limit_bytes=VMEM_LIMIT),
        name="mem_kv",
    )(mem, g, wk, wv)


def _mix_in_kernel(x_ref, g_ref, w_ref, wf_ref, fb_ref, poolw_ref, pools_ref,
                   dww_ref, dwb_ref, lng_ref, lnb_ref, pw_ref, scw_ref, gn_ref,
                   y_ref, q_ref, k_ref, v_ref, ccol_ref, crow_ref,
                   pbuf, cbuf, sbuf, carry_ref):
    i = pl.program_id(0)
    tm = x_ref.shape[0]

    @pl.when(i == 0)
    def _():
        pbuf[0:HALO, :] = jnp.zeros((HALO, C_GRP), F32)
        cbuf[0:HALO, :] = jnp.zeros((HALO, C_GRP), F32)
        sbuf[0:HALO, :] = jnp.zeros((HALO, C_GRP), F32)
        carry_ref[...] = jnp.zeros_like(carry_ref)

    h = _rms(x_ref[...], g_ref[...]).astype(BF16)

    def proj(s):
        return _dot(h, w_ref[:, s * C_GRP:(s + 1) * C_GRP])

    u = proj(0)
    pbuf[HALO:HALO + tm, :] = u
    pos = i * tm + lax.broadcasted_iota(jnp.int32, (tm, 1), 0)
    ya = []
    for gi, w in enumerate(POOL_WINDOWS):
        sl = slice(gi * POOL_GROUP, (gi + 1) * POOL_GROUP)
        tot = u[:, sl]
        for d in range(1, w):
            tot = tot + pbuf[HALO - d:HALO - d + tm, sl]
        count = jnp.minimum(pos + 1, w).astype(F32)
        p = (tot / count - u[:, sl]).astype(BF16)
        ya.append(_dot(p, poolw_ref[gi]))
    ya = jnp.concatenate(ya, axis=-1) * pools_ref[...]
    y_ref[:, 0:C_GRP] = _rms(ya, gn_ref[:, 0:C_GRP]).astype(BF16)
    pbuf[0:HALO, :] = pbuf[tm:tm + HALO, :]

    glu = proj(1) * jax.nn.sigmoid(proj(2))
    cbuf[HALO:HALO + tm, :] = glu
    conv = dww_ref[CONV_K - 1:CONV_K, :] * glu
    for kk in range(CONV_K - 1):
        off = HALO - (CONV_K - 1) + kk
        conv = conv + dww_ref[kk:kk + 1, :] * cbuf[off:off + tm, :]
    conv = conv + dwb_ref[...]
    mu = jnp.mean(conv, axis=-1, keepdims=True)
    xc = conv - mu
    var = jnp.mean(xc * xc, axis=-1, keepdims=True)
    ln = xc * lax.rsqrt(var + EPS) * lng_ref[...] + lnb_ref[...]
    act = (ln * jax.nn.sigmoid(ln)).astype(BF16)
    yb = _dot(act, pw_ref[...])
    y_ref[:, C_GRP:2 * C_GRP] = _rms(yb, gn_ref[:, C_GRP:2 * C_GRP]).astype(BF16)
    cbuf[0:HALO, :] = cbuf[tm:tm + HALO, :]

    q_ref[...] = (proj(3) * ATTN_SCALE).astype(BF16)
    k_ref[...] = proj(4).astype(BF16)
    v_ref[...] = proj(5).astype(BF16)
    f = _dot(h, wf_ref[...]) + fb_ref[...]
    logf = jnp.minimum(f, 0.0) - jnp.log(1.0 + jnp.exp(-jnp.abs(f)))
    tri = (lax.broadcasted_iota(jnp.int32, (tm, tm), 0)
           >= lax.broadcasted_iota(jnp.int32, (tm, tm), 1)).astype(F32)
    c = jnp.dot(tri, logf, precision=lax.Precision.HIGHEST,
                preferred_element_type=F32) + carry_ref[...]
    carry_ref[...] = c[tm - 1:tm, :]
    ccol_ref[...] = c
    crow_ref[...] = c.T[0:8, :]

    sb = proj(6)
    sv = proj(7) * proj(8)
    sbuf[HALO:HALO + tm, :] = sv
    sconv = scw_ref[SC_K - 1:SC_K, :] * sv
    for kk in range(SC_K - 1):
        off = HALO - (SC_K - 1) + kk
        sconv = sconv + scw_ref[kk:kk + 1, :] * sbuf[off:off + tm, :]
    yd = sb * sconv
    y_ref[:, 2 * C_GRP:3 * C_GRP] = _rms(yd, gn_ref[:, 3 * C_GRP:4 * C_GRP]).astype(BF16)
    sbuf[0:HALO, :] = sbuf[tm:tm + HALO, :]


def _mix_in(x, g, w_main, w_f, f_bias, pool_w, pool_scale, dw_w, dw_b, ln_g, ln_b, pw, sc_w, gn):
    tm = MIX_TM
    row = lambda i: (i, 0)
    n_sec = w_main.shape[1] // C_GRP
    out_shape = [
        jax.ShapeDtypeStruct((SEQ, 3 * C_GRP), BF16),
        jax.ShapeDtypeStruct((SEQ, C_GRP), BF16),
        jax.ShapeDtypeStruct((SEQ, C_GRP), BF16),
        jax.ShapeDtypeStruct((SEQ, C_GRP), BF16),
        jax.ShapeDtypeStruct((SEQ, LANES), F32),
        jax.ShapeDtypeStruct((8, SEQ), F32),
    ]
    out_specs = [
        pl.BlockSpec((tm, 3 * C_GRP), row),
        pl.BlockSpec((tm, C_GRP), row),
        pl.BlockSpec((tm, C_GRP), row),
        pl.BlockSpec((tm, C_GRP), row),
        pl.BlockSpec((tm, LANES), row),
        pl.BlockSpec((8, tm), lambda i: (0, i)),
    ]
    in_specs = [
        pl.BlockSpec((tm, D_MODEL), row),
        _const_spec((1, D_MODEL)),
        _const_spec((D_MODEL, n_sec * C_GRP)),
        _const_spec((D_MODEL, LANES)),
        _const_spec((1, LANES)),
        _const_spec((len(POOL_WINDOWS), POOL_GROUP, POOL_GROUP)),
        _const_spec((1, C_GRP)),
        _const_spec((CONV_K, C_GRP)),
        _const_spec((1, C_GRP)),
        _const_spec((1, C_GRP)),
        _const_spec((1, C_GRP)),
        _const_spec((C_GRP, C_GRP)),
        _const_spec((SC_K, C_GRP)),
        _const_spec((1, 4 * C_GRP)),
    ]
    return pl.pallas_call(
        _mix_in_kernel,
        out_shape=out_shape,
        grid=(SEQ // tm,),
        in_specs=in_specs,
        out_specs=out_specs,
        scratch_shapes=[pltpu.VMEM((HALO + tm, C_GRP), F32)] * 3 + [pltpu.VMEM((1, LANES), F32)],
        compiler_params=_params("arbitrary"),
        name="mix_in",
    )(x, g, w_main, w_f, f_bias, pool_w, pool_scale, dw_w, dw_b, ln_g, ln_b, pw, sc_w, gn)


def _fox_kernel(q_ref, k_ref, v_ref, cq_ref, ck_ref, g_ref, o_ref, m_ref, l_ref, acc_ref):
    i = pl.program_id(0)
    j = pl.program_id(1)
    t = q_ref.shape[0]

    @pl.when(j == 0)
    def _():
        m_ref[...] = jnp.full_like(m_ref, NEG_BIG)
        l_ref[...] = jnp.zeros_like(l_ref)
        acc_ref[...] = jnp.zeros_like(acc_ref)

    @pl.when(j <= i)
    def _():
        rows = i * t + lax.broadcasted_iota(jnp.int32, (t, t), 0)
        cols = j * t + lax.broadcasted_iota(jnp.int32, (t, t), 1)
        keep = cols <= rows
        for hd in range(HEADS):
            sl = slice(hd * HEAD_DIM, (hd + 1) * HEAD_DIM)
            s = _dot_nt(q_ref[:, sl], k_ref[:, sl])
            s = s + (cq_ref[:, hd:hd + 1] - ck_ref[hd:hd + 1, :])
            s = jnp.where(keep, s, NEG_BIG)
            m_old = m_ref[hd]
            m_new = jnp.maximum(m_old, jnp.max(s, axis=-1, keepdims=True))
            alpha = jnp.exp(m_old - m_new)
            p = jnp.exp(s - m_new)
            l_ref[hd] = alpha * l_ref[hd] + jnp.sum(p, axis=-1, keepdims=True)
            acc_ref[:, sl] = alpha * acc_ref[:, sl] + _dot(p.astype(BF16), v_ref[:, sl])
            m_ref[hd] = m_new

    @pl.when(j == i)
    def _():
        o = jnp.concatenate(
            [acc_ref[:, hd * HEAD_DIM:(hd + 1) * HEAD_DIM] / l_ref[hd] for hd in range(HEADS)],
            axis=-1)
        o_ref[...] = _rms(o, g_ref[...]).astype(BF16)


def _fox(q, k, v, ccol, crow, g):
    t = ATT_T
    n = SEQ // t
    kv = lambda i, j: (jnp.minimum(j, i), 0)
    return pl.pallas_call(
        _fox_kernel,
        out_shape=jax.ShapeDtypeStruct((SEQ, C_GRP), BF16),
        grid=(n, n),
        in_specs=[
            pl.BlockSpec((t, C_GRP), lambda i, j: (i, 0)),
            pl.BlockSpec((t, C_GRP), kv),
            pl.BlockSpec((t, C_GRP), kv),
            pl.BlockSpec((t, LANES), lambda i, j: (i, 0)),
            pl.BlockSpec((8, t), lambda i, j: (0, jnp.minimum(j, i))),
            _const_spec((1, C_GRP)),
        ],
        out_specs=pl.BlockSpec((t, C_GRP), lambda i, j: (i, 0)),
        scratch_shapes=[
            pltpu.VMEM((HEADS, t, 1), F32),
            pltpu.VMEM((HEADS, t, 1), F32),
            pltpu.VMEM((t, C_GRP), F32),
        ],
        compiler_params=_params("arbitrary", "arbitrary"),
        name="fox_attn",
    )(q, k, v, ccol, crow, g)


def _mix_out_kernel(x_ref, yabd_ref, yc_ref, wo_ref, g_ref, wq_ref, km_ref, vm_ref, wxo_ref, o_ref):
    x2 = x_ref[...]
    x2 = x2 + _dot(yabd_ref[:, 0:C_GRP], wo_ref[0:C_GRP, :])
    x2 = x2 + _dot(yabd_ref[:, C_GRP:2 * C_GRP], wo_ref[C_GRP:2 * C_GRP, :])
    x2 = x2 + _dot(yc_ref[...], wo_ref[2 * C_GRP:3 * C_GRP, :])
    x2 = x2 + _dot(yabd_ref[:, 2 * C_GRP:3 * C_GRP], wo_ref[3 * C_GRP:4 * C_GRP, :])

    h = _rms(x2, g_ref[...]).astype(BF16)
    q = (_dot(h, wq_ref[...]) * ATTN_SCALE).astype(BF16)
    outs = []
    for hd in range(HEADS):
        sl = slice(hd * HEAD_DIM, (hd + 1) * HEAD_DIM)
        s = _dot_nt(q[:, sl], km_ref[:, sl])
        e = jnp.exp(s - jnp.max(s, axis=-1, keepdims=True))
        p = (e / jnp.sum(e, axis=-1, keepdims=True)).astype(BF16)
        outs.append(_dot(p, vm_ref[:, sl]))
    o = jnp.concatenate(outs, axis=-1).astype(BF16)
    o_ref[...] = x2 + _dot(o, wxo_ref[...])


def _mix_out(x, yabd, yc, w_out, g, wq, km, vm, wxo):
    tm = OUT_TM
    row = lambda i: (i, 0)
    return pl.pallas_call(
        _mix_out_kernel,
        out_shape=jax.ShapeDtypeStruct((SEQ, D_MODEL), F32),
        grid=(SEQ // tm,),
        in_specs=[
            pl.BlockSpec((tm, D_MODEL), row),
            pl.BlockSpec((tm, 3 * C_GRP), row),
            pl.BlockSpec((tm, C_GRP), row),
            _const_spec((D_MODEL, D_MODEL)),
            _const_spec((1, D_MODEL)),
            _const_spec((D_MODEL, C_GRP)),
            _const_spec((N_MEM, C_GRP)),
            _const_spec((N_MEM, C_GRP)),
            _const_spec((C_GRP, D_MODEL)),
        ],
        out_specs=pl.BlockSpec((tm, D_MODEL), row),
        compiler_params=_params("arbitrary"),
        name="mix_out_xattn",
    )(x, yabd, yc, w_out, g, wq, km, vm, wxo)


def kernel(x, mem, ffn1_norm, ffn1_w_gate, ffn1_w_up, ffn1_w_down, mix_norm, w_mix_in, pool_w, pool_scale, conv_dw_w, conv_dw_b, conv_ln_g, conv_ln_b, conv_pw, fox_f_bias, sc_w, grp_norm, w_mix_out, xa_norm, mem_norm, w_xq, w_xk, w_xv, w_xo, ffn2_norm, ffn2_w_gate, ffn2_w_up, ffn2_w_down, final_norm):
    xs = x.reshape(SEQ, D_MODEL)
    mem2 = mem.reshape(N_MEM, D_MODEL)
    bf = lambda a: a.astype(BF16)
    vec = lambda a: a.reshape(1, -1)
    f_off = 6 * C_GRP
    for l in range(DEPTH):
        xs = _ffn(xs, vec(ffn1_norm[l]), bf(ffn1_w_gate[l]), bf(ffn1_w_up[l]), bf(ffn1_w_down[l]))

        w_in = w_mix_in[l]
        w_main = bf(jnp.concatenate([w_in[:, :f_off], w_in[:, f_off + HEADS:]], axis=1))
        w_f = bf(jnp.pad(w_in[:, f_off:f_off + HEADS], ((0, 0), (0, LANES - HEADS))))
        f_bias = jnp.pad(fox_f_bias[l], (0, LANES - HEADS)).reshape(1, LANES)
        yabd, q, k, v, ccol, crow = _mix_in(
            xs, vec(mix_norm[l]), w_main, w_f, f_bias, bf(pool_w[l]), vec(pool_scale[l]),
            conv_dw_w[l], vec(conv_dw_b[l]), vec(conv_ln_g[l]), vec(conv_ln_b[l]),
            bf(conv_pw[l]), sc_w[l], vec(grp_norm[l]))
        yc = _fox(q, k, v, ccol, crow, vec(grp_norm[l, 2 * C_GRP:3 * C_GRP]))

        km, vm = _mem_kv(mem2, vec(mem_norm[l]), bf(w_xk[l]), bf(w_xv[l]))
        xs = _mix_out(xs, yabd, yc, bf(w_mix_out[l]), vec(xa_norm[l]), bf(w_xq[l]), km, vm, bf(w_xo[l]))

        gf = vec(final_norm) if l == DEPTH - 1 else None
        xs = _ffn(xs, vec(ffn2_norm[l]), bf(ffn2_w_gate[l]), bf(ffn2_w_up[l]), bf(ffn2_w_down[l]), gf)
    return xs.reshape(1, SEQ, D_MODEL)
```

```python
import functools

import jax
import jax.numpy as jnp
from jax import lax
from jax.experimental import pallas as pl
from jax.experimental.pallas import tpu as pltpu

F32 = jnp.float32
BF16 = jnp.bfloat16

D_MODEL = 2048
SEQ = 16384
DEPTH = 2
C_GRP = 512
POOL_WINDOWS = (2, 4, 8, 16)
POOL_GROUP = 128
CONV_K = 31
SC_K = 3
HEADS = 4
HEAD_DIM = 128
N_MEM = 256
D_FF = 5632
EPS = 1e-6
ATTN_SCALE = HEAD_DIM ** -0.5
LOG2E = 1.4426950408889634
N_SPLIT = 3
ACC_ROWS = HEAD_DIM + 16
LANES = 128
HALO = 32
NEG_BIG = -1e30
VMEM_LIMIT = 56 * 1024 * 1024

FFN_TM, FFN_TF = 512, 512
MIX_TM = 256
ATT_T = 512
OUT_TM = 256


def _rms(x, g):
    ms = jnp.mean(x * x, axis=-1, keepdims=True)
    return x * lax.rsqrt(ms + EPS) * g


def _dot(a, b):
    return jnp.dot(a, b, preferred_element_type=F32)


def _dot_nt(a, b):
    return lax.dot_general(a, b, (((1,), (1,)), ((), ())), preferred_element_type=F32)


def _const_spec(shape):
    return pl.BlockSpec(shape, lambda *_: (0,) * len(shape), pipeline_mode=pl.Buffered(1))


def _params(*sem):
    return pltpu.CompilerParams(dimension_semantics=sem, vmem_limit_bytes=VMEM_LIMIT)


def _ffn_kernel(*refs, final):
    if final:
        x_ref, g_ref, wg_ref, wu_ref, wd_ref, gf_ref, o_ref, h_ref = refs
    else:
        x_ref, g_ref, wg_ref, wu_ref, wd_ref, o_ref, h_ref = refs
    j = pl.program_id(1)

    @pl.when(j == 0)
    def _():
        h_ref[...] = _rms(x_ref[...], g_ref[...]).astype(BF16)
        o_ref[...] = jnp.zeros_like(o_ref)

    h = h_ref[...]
    gate = _dot(h, wg_ref[...])
    up = _dot(h, wu_ref[...])
    a = (gate * jax.nn.sigmoid(gate) * up).astype(BF16)
    o_ref[...] += _dot(a, wd_ref[...])

    @pl.when(j == pl.num_programs(1) - 1)
    def _():
        y = x_ref[...] + 0.5 * o_ref[...]
        if final:
            y = _rms(y, gf_ref[...])
        o_ref[...] = y


def _ffn(x, g, wg, wu, wd, gf=None):
    final = gf is not None
    tm, tf = FFN_TM, FFN_TF
    in_specs = [
        pl.BlockSpec((tm, D_MODEL), lambda i, j: (i, 0)),
        _const_spec((1, D_MODEL)),
        pl.BlockSpec((D_MODEL, tf), lambda i, j: (0, j)),
        pl.BlockSpec((D_MODEL, tf), lambda i, j: (0, j)),
        pl.BlockSpec((tf, D_MODEL), lambda i, j: (j, 0)),
    ]
    args = [x, g, wg, wu, wd]
    if final:
        in_specs.append(_const_spec((1, D_MODEL)))
        args.append(gf)
    return pl.pallas_call(
        functools.partial(_ffn_kernel, final=final),
        out_shape=jax.ShapeDtypeStruct((SEQ, D_MODEL), F32),
        grid=(SEQ // tm, D_FF // tf),
        in_specs=in_specs,
        out_specs=pl.BlockSpec((tm, D_MODEL), lambda i, j: (i, 0)),
        scratch_shapes=[pltpu.VMEM((tm, D_MODEL), BF16)],
        compiler_params=_params("arbitrary", "arbitrary"),
        name="ffn_final" if final else "ffn",
    )(*args)


def _mem_kv_kernel(mem_ref, g_ref, wk_ref, wv_ref, k_ref, v_ref):
    mn = _rms(mem_ref[...], g_ref[...]).astype(BF16)
    k_ref[...] = _dot(mn, wk_ref[...]).astype(BF16)
    v_ref[...] = _dot(mn, wv_ref[...]).astype(BF16)


def _mem_kv(mem, g, wk, wv):
    return pl.pallas_call(
        _mem_kv_kernel,
        out_shape=[jax.ShapeDtypeStruct((N_MEM, C_GRP), BF16)] * 2,
        compiler_params=pltpu.CompilerParams(vmem_limit_bytes=VMEM_LIMIT),
        name="mem_kv",
    )(mem, g, wk, wv)


def _mix_in_kernel(x_ref, g_ref, w_ref, wf_ref, fb_ref, poolw_ref, pools_ref,
                   dww_ref, dwb_ref, lng_ref, lnb_ref, pw_ref, scw_ref, gn_ref,
                   y_ref, qa_ref, ka_ref, vt_ref,
                   pbuf, cbuf, sbuf, carry_ref):
    i = pl.program_id(0)
    tm = x_ref.shape[0]

    @pl.when(i == 0)
    def _():
        pbuf[0:HALO, :] = jnp.zeros((HALO, C_GRP), F32)
        cbuf[0:HALO, :] = jnp.zeros((HALO, C_GRP), F32)
        sbuf[0:HALO, :] = jnp.zeros((HALO, C_GRP), F32)
        carry_ref[...] = jnp.zeros_like(carry_ref)

    h = _rms(x_ref[...], g_ref[...]).astype(BF16)

    def proj(s):
        return _dot(h, w_ref[:, s * C_GRP:(s + 1) * C_GRP])

    u = proj(0)
    pbuf[HALO:HALO + tm, :] = u
    pos = i * tm + lax.broadcasted_iota(jnp.int32, (tm, 1), 0)
    ya = []
    for gi, w in enumerate(POOL_WINDOWS):
        sl = slice(gi * POOL_GROUP, (gi + 1) * POOL_GROUP)
        tot = u[:, sl]
        for d in range(1, w):
            tot = tot + pbuf[HALO - d:HALO - d + tm, sl]
        count = jnp.minimum(pos + 1, w).astype(F32)
        p = (tot / count - u[:, sl]).astype(BF16)
        ya.append(_dot(p, poolw_ref[gi]))
    ya = jnp.concatenate(ya, axis=-1) * pools_ref[...]
    y_ref[:, 0:C_GRP] = _rms(ya, gn_ref[:, 0:C_GRP]).astype(BF16)
    pbuf[0:HALO, :] = pbuf[tm:tm + HALO, :]

    glu = proj(1) * jax.nn.sigmoid(proj(2))
    cbuf[HALO:HALO + tm, :] = glu
    conv = dww_ref[CONV_K - 1:CONV_K, :] * glu
    for kk in range(CONV_K - 1):
        off = HALO - (CONV_K - 1) + kk
        conv = conv + dww_ref[kk:kk + 1, :] * cbuf[off:off + tm, :]
    conv = conv + dwb_ref[...]
    mu = jnp.mean(conv, axis=-1, keepdims=True)
    xc = conv - mu
    var = jnp.mean(xc * xc, axis=-1, keepdims=True)
    ln = xc * lax.rsqrt(var + EPS) * lng_ref[...] + lnb_ref[...]
    act = (ln * jax.nn.sigmoid(ln)).astype(BF16)
    yb = _dot(act, pw_ref[...])
    y_ref[:, C_GRP:2 * C_GRP] = _rms(yb, gn_ref[:, C_GRP:2 * C_GRP]).astype(BF16)
    cbuf[0:HALO, :] = cbuf[tm:tm + HALO, :]

    vt_ref[...] = proj(5).T.astype(BF16)
    f = _dot(h, wf_ref[...]) + fb_ref[...]
    logf = jnp.minimum(f, 0.0) - jnp.log(1.0 + jnp.exp(-jnp.abs(f)))
    tri = (lax.broadcasted_iota(jnp.int32, (tm, tm), 0)
           >= lax.broadcasted_iota(jnp.int32, (tm, tm), 1)).astype(F32)
    c = jnp.dot(tri, logf, precision=lax.Precision.HIGHEST,
                preferred_element_type=F32) + carry_ref[...]
    carry_ref[...] = c[tm - 1:tm, :]
    c2 = c * LOG2E
    qs = proj(3) * (ATTN_SCALE * LOG2E)
    ks = proj(4)
    lane = lax.broadcasted_iota(jnp.int32, (tm, LANES), 1)
    ones = jnp.where(lane < 2 * N_SPLIT, 1.0, 0.0)
    for hd in range(HEADS):
        ch = c2[:, hd:hd + 1]
        p1 = ch.astype(BF16).astype(F32)
        r1 = ch - p1
        p2 = r1.astype(BF16).astype(F32)
        p3 = r1 - p2
        qx = jnp.where(lane == 0, p1, jnp.where(lane == 1, p2, jnp.where(lane == 2, p3, ones)))
        kx = jnp.where(lane == 3, -p1, jnp.where(lane == 4, -p2, jnp.where(lane == 5, -p3, ones)))
        sl = slice(hd * HEAD_DIM, (hd + 1) * HEAD_DIM)
        base = hd * 2 * HEAD_DIM
        qa_ref[:, base:base + HEAD_DIM] = qs[:, sl].astype(BF16)
        qa_ref[:, base + HEAD_DIM:base + 2 * HEAD_DIM] = qx.astype(BF16)
        ka_ref[:, base:base + HEAD_DIM] = ks[:, sl].astype(BF16)
        ka_ref[:, base + HEAD_DIM:base + 2 * HEAD_DIM] = kx.astype(BF16)

    sb = proj(6)
    sv = proj(7) * proj(8)
    sbuf[HALO:HALO + tm, :] = sv
    sconv = scw_ref[SC_K - 1:SC_K, :] * sv
    for kk in range(SC_K - 1):
        off = HALO - (SC_K - 1) + kk
        sconv = sconv + scw_ref[kk:kk + 1, :] * sbuf[off:off + tm, :]
    yd = sb * sconv
    y_ref[:, 2 * C_GRP:3 * C_GRP] = _rms(yd, gn_ref[:, 3 * C_GRP:4 * C_GRP]).astype(BF16)
    sbuf[0:HALO, :] = sbuf[tm:tm + HALO, :]


def _mix_in(x, g, w_main, w_f, f_bias, pool_w, pool_scale, dw_w, dw_b, ln_g, ln_b, pw, sc_w, gn):
    tm = MIX_TM
    row = lambda i: (i, 0)
    n_sec = w_main.shape[1] // C_GRP
    out_shape = [
        jax.ShapeDtypeStruct((SEQ, 3 * C_GRP), BF16),
        jax.ShapeDtypeStruct((SEQ, 2 * C_GRP), BF16),
        jax.ShapeDtypeStruct((SEQ, 2 * C_GRP), BF16),
        jax.ShapeDtypeStruct((C_GRP, SEQ), BF16),
    ]
    out_specs = [
        pl.BlockSpec((tm, 3 * C_GRP), row),
        pl.BlockSpec((tm, 2 * C_GRP), row),
        pl.BlockSpec((tm, 2 * C_GRP), row),
        pl.BlockSpec((C_GRP, tm), lambda i: (0, i)),
    ]
    in_specs = [
        pl.BlockSpec((tm, D_MODEL), row),
        _const_spec((1, D_MODEL)),
        _const_spec((D_MODEL, n_sec * C_GRP)),
        _const_spec((D_MODEL, LANES)),
        _const_spec((1, LANES)),
        _const_spec((len(POOL_WINDOWS), POOL_GROUP, POOL_GROUP)),
        _const_spec((1, C_GRP)),
        _const_spec((CONV_K, C_GRP)),
        _const_spec((1, C_GRP)),
        _const_spec((1, C_GRP)),
        _const_spec((1, C_GRP)),
        _const_spec((C_GRP, C_GRP)),
        _const_spec((SC_K, C_GRP)),
        _const_spec((1, 4 * C_GRP)),
    ]
    return pl.pallas_call(
        _mix_in_kernel,
        out_shape=out_shape,
        grid=(SEQ // tm,),
        in_specs=in_specs,
        out_specs=out_specs,
        scratch_shapes=[pltpu.VMEM((HALO + tm, C_GRP), F32)] * 3 + [pltpu.VMEM((1, LANES), F32)],
        compiler_params=_params("arbitrary"),
        name="mix_in",
    )(x, g, w_main, w_f, f_bias, pool_w, pool_scale, dw_w, dw_b, ln_g, ln_b, pw, sc_w, gn)


def _fox_kernel(qi_ref, ki_ref, qa_ref, ka_ref, vt_ref, g_ref, o_ref, m_ref, acc_ref):
    step = pl.program_id(0)
    qi = qi_ref[step]
    ki = ki_ref[step]
    t = qa_ref.shape[0]

    @pl.when(ki == 0)
    def _():
        m_ref[...] = jnp.full_like(m_ref, NEG_BIG)
        acc_ref[...] = jnp.zeros_like(acc_ref)

    def accumulate(diagonal):
        ones_rows = (lax.broadcasted_iota(jnp.int32, (ACC_ROWS - HEAD_DIM, t), 0) == 0).astype(BF16)
        if diagonal:
            keep = (lax.broadcasted_iota(jnp.int32, (t, t), 0)
                    <= lax.broadcasted_iota(jnp.int32, (t, t), 1))
        sts, pts, alphas = {}, {}, {}

        def scores(hd):
            wide = slice(hd * 2 * HEAD_DIM, (hd + 1) * 2 * HEAD_DIM)
            st = _dot_nt(ka_ref[:, wide], qa_ref[:, wide])
            sts[hd] = jnp.where(keep, st, NEG_BIG) if diagonal else st

        def softmax(hd):
            m_old = m_ref[hd]
            m_new = jnp.maximum(m_old, jnp.max(sts[hd], axis=0, keepdims=True))
            alphas[hd] = jnp.exp2(m_old - m_new)
            pts[hd] = jnp.exp2(sts[hd] - m_new).astype(BF16)
            m_ref[hd] = m_new

        def values(hd):
            vat = jnp.concatenate([vt_ref[hd * HEAD_DIM:(hd + 1) * HEAD_DIM, :], ones_rows], axis=0)
            acc_ref[hd] = alphas[hd] * acc_ref[hd] + _dot(vat, pts[hd])

        for hd in range(HEADS + 2):
            if hd < HEADS:
                scores(hd)
            if 0 <= hd - 1 < HEADS:
                softmax(hd - 1)
            if 0 <= hd - 2 < HEADS:
                values(hd - 2)

    @pl.when(ki < qi)
    def _():
        accumulate(diagonal=False)

    @pl.when(ki == qi)
    def _():
        accumulate(diagonal=True)
        ot = jnp.concatenate(
            [acc_ref[hd, 0:HEAD_DIM, :] / acc_ref[hd, HEAD_DIM:HEAD_DIM + 1, :] for hd in range(HEADS)],
            axis=0)
        ms = jnp.mean(ot * ot, axis=0, keepdims=True)
        o_ref[...] = (ot * lax.rsqrt(ms + EPS) * g_ref[...]).T.astype(BF16)


def _fox(qa, ka, vt, g):
    t = ATT_T
    n = SEQ // t
    pairs = [(i, j) for i in range(n) for j in range(i + 1)]
    qi = jnp.array([p[0] for p in pairs], jnp.int32)
    ki = jnp.array([p[1] for p in pairs], jnp.int32)
    qmap = lambda s, qi, ki: (qi[s], 0)
    kmap = lambda s, qi, ki: (ki[s], 0)
    return pl.pallas_call(
        _fox_kernel,
        out_shape=jax.ShapeDtypeStruct((SEQ, C_GRP), BF16),
        grid_spec=pltpu.PrefetchScalarGridSpec(
            num_scalar_prefetch=2,
            grid=(len(pairs),),
            in_specs=[
                pl.BlockSpec((t, 2 * C_GRP), qmap),
                pl.BlockSpec((t, 2 * C_GRP), kmap),
                pl.BlockSpec((C_GRP, t), lambda s, qi, ki: (0, ki[s])),
                pl.BlockSpec((C_GRP, 1), lambda s, qi, ki: (0, 0)),
            ],
            out_specs=pl.BlockSpec((t, C_GRP), qmap),
            scratch_shapes=[
                pltpu.VMEM((HEADS, 1, t), F32),
                pltpu.VMEM((HEADS, ACC_ROWS, t), F32),
            ],
        ),
        compiler_params=_params("arbitrary"),
        name="fox_attn",
    )(qi, ki, qa, ka, vt, g)


def _mix_out_kernel(x_ref, yabd_ref, yc_ref, wo_ref, g_ref, wq_ref, km_ref, vm_ref, wxo_ref, o_ref):
    x2 = x_ref[...]
    x2 = x2 + _dot(yabd_ref[:, 0:C_GRP], wo_ref[0:C_GRP, :])
    x2 = x2 + _dot(yabd_ref[:, C_GRP:2 * C_GRP], wo_ref[C_GRP:2 * C_GRP, :])
    x2 = x2 + _dot(yc_ref[...], wo_ref[2 * C_GRP:3 * C_GRP, :])
    x2 = x2 + _dot(yabd_ref[:, 2 * C_GRP:3 * C_GRP], wo_ref[3 * C_GRP:4 * C_GRP, :])

    h = _rms(x2, g_ref[...]).astype(BF16)
    q = (_dot(h, wq_ref[...]) * ATTN_SCALE).astype(BF16)
    outs = []
    for hd in range(HEADS):
        sl = slice(hd * HEAD_DIM, (hd + 1) * HEAD_DIM)
        s = _dot_nt(q[:, sl], km_ref[:, sl])
        e = jnp.exp(s - jnp.max(s, axis=-1, keepdims=True))
        p = (e / jnp.sum(e, axis=-1, keepdims=True)).astype(BF16)
        outs.append(_dot(p, vm_ref[:, sl]))
    o = jnp.concatenate(outs, axis=-1).astype(BF16)
    o_ref[...] = x2 + _dot(o, wxo_ref[...])


def _mix_out(x, yabd, yc, w_out, g, wq, km, vm, wxo):
    tm = OUT_TM
    row = lambda i: (i, 0)
    return pl.pallas_call(
        _mix_out_kernel,
        out_shape=jax.ShapeDtypeStruct((SEQ, D_MODEL), F32),
        grid=(SEQ // tm,),
        in_specs=[
            pl.BlockSpec((tm, D_MODEL), row),
            pl.BlockSpec((tm, 3 * C_GRP), row),
            pl.BlockSpec((tm, C_GRP), row),
            _const_spec((D_MODEL, D_MODEL)),
            _const_spec((1, D_MODEL)),
            _const_spec((D_MODEL, C_GRP)),
            _const_spec((N_MEM, C_GRP)),
            _const_spec((N_MEM, C_GRP)),
            _const_spec((C_GRP, D_MODEL)),
        ],
        out_specs=pl.BlockSpec((tm, D_MODEL), row),
        compiler_params=_params("arbitrary"),
        name="mix_out_xattn",
    )(x, yabd, yc, w_out, g, wq, km, vm, wxo)


def kernel(x, mem, ffn1_norm, ffn1_w_gate, ffn1_w_up, ffn1_w_down, mix_norm, w_mix_in, pool_w, pool_scale, conv_dw_w, conv_dw_b, conv_ln_g, conv_ln_b, conv_pw, fox_f_bias, sc_w, grp_norm, w_mix_out, xa_norm, mem_norm, w_xq, w_xk, w_xv, w_xo, ffn2_norm, ffn2_w_gate, ffn2_w_up, ffn2_w_down, final_norm):
    xs = x.reshape(SEQ, D_MODEL)
    mem2 = mem.reshape(N_MEM, D_MODEL)
    bf = lambda a: a.astype(BF16)
    vec = lambda a: a.reshape(1, -1)
    f_off = 6 * C_GRP
    for l in range(DEPTH):
        xs = _ffn(xs, vec(ffn1_norm[l]), bf(ffn1_w_gate[l]), bf(ffn1_w_up[l]), bf(ffn1_w_down[l]))

        w_in = w_mix_in[l]
        w_main = bf(jnp.concatenate([w_in[:, :f_off], w_in[:, f_off + HEADS:]], axis=1))
        w_f = bf(jnp.pad(w_in[:, f_off:f_off + HEADS], ((0, 0), (0, LANES - HEADS))))
        f_bias = jnp.pad(fox_f_bias[l], (0, LANES - HEADS)).reshape(1, LANES)
        yabd, qa, ka, vt = _mix_in(
            xs, vec(mix_norm[l]), w_main, w_f, f_bias, bf(pool_w[l]), vec(pool_scale[l]),
            conv_dw_w[l], vec(conv_dw_b[l]), vec(conv_ln_g[l]), vec(conv_ln_b[l]),
            bf(conv_pw[l]), sc_w[l], vec(grp_norm[l]))
        yc = _fox(qa, ka, vt, grp_norm[l, 2 * C_GRP:3 * C_GRP].reshape(C_GRP, 1))

        km, vm = _mem_kv(mem2, vec(mem_norm[l]), bf(w_xk[l]), bf(w_xv[l]))
        xs = _mix_out(xs, yabd, yc, bf(w_mix_out[l]), vec(xa_norm[l]), bf(w_xq[l]), km, vm, bf(w_xo[l]))

        gf = vec(final_norm) if l == DEPTH - 1 else None
        xs = _ffn(xs, vec(ffn2_norm[l]), bf(ffn2_w_gate[l]), bf(ffn2_w_up[l]), bf(ffn2_w_down[l]), gf)
    return xs.reshape(1, SEQ, D_MODEL)
```

```python
import functools

import jax
import jax.numpy as jnp
from jax import lax
from jax.experimental import pallas as pl
from jax.experimental.pallas import tpu as pltpu

F32 = jnp.float32
BF16 = jnp.bfloat16

D_MODEL = 2048
SEQ = 16384
DEPTH = 2
C_GRP = 512
POOL_WINDOWS = (2, 4, 8, 16)
POOL_GROUP = 128
CONV_K = 31
SC_K = 3
HEADS = 4
HEAD_DIM = 128
N_MEM = 256
D_FF = 5632
EPS = 1e-6
ATTN_SCALE = HEAD_DIM ** -0.5
LOG2E = 1.4426950408889634
N_SPLIT = 3
ACC_ROWS = HEAD_DIM + 16
LANES = 128
SUBLANES = 8
HALO = 32
NEG_BIG = -1e30
VMEM_LIMIT = 56 * 1024 * 1024

FFN_TM, FFN_TF = 1024, 512
MIX_TM = 256
ATT_T = 512
OUT_TM = 256


def _rms(x, g):
    ms = jnp.mean(x * x, axis=-1, keepdims=True)
    return x * lax.rsqrt(ms + EPS) * g


def _dot(a, b):
    return jnp.dot(a, b, preferred_element_type=F32)


def _dot_nt(a, b):
    return lax.dot_general(a, b, (((1,), (1,)), ((), ())), preferred_element_type=F32)


def _const_spec(shape):
    return pl.BlockSpec(shape, lambda *_: (0,) * len(shape), pipeline_mode=pl.Buffered(1))


def _params(*sem):
    return pltpu.CompilerParams(dimension_semantics=sem, vmem_limit_bytes=VMEM_LIMIT)


def _ffn_kernel(*refs, final):
    if final:
        x_ref, g_ref, wg_ref, wu_ref, wd_ref, gf_ref, o_ref, h_ref = refs
    else:
        x_ref, g_ref, wg_ref, wu_ref, wd_ref, o_ref, h_ref = refs
    j = pl.program_id(1)

    @pl.when(j == 0)
    def _():
        x = x_ref[...]
        h_ref[...] = _rms(x, g_ref[...]).astype(BF16)
        o_ref[...] = x

    h = h_ref[...]
    gate = _dot(h, wg_ref[...])
    up = _dot(h, wu_ref[...])
    a = (0.5 * gate * jax.nn.sigmoid(gate) * up).astype(BF16)
    o_ref[...] += _dot(a, wd_ref[...])

    if final:
        @pl.when(j == pl.num_programs(1) - 1)
        def _():
            o_ref[...] = _rms(o_ref[...], gf_ref[...])


def _ffn(x, g, wg, wu, wd, gf=None):
    final = gf is not None
    tm, tf = FFN_TM, FFN_TF
    in_specs = [
        pl.BlockSpec((tm, D_MODEL), lambda i, j: (i, 0)),
        _const_spec((1, D_MODEL)),
        pl.BlockSpec((D_MODEL, tf), lambda i, j: (0, j)),
        pl.BlockSpec((D_MODEL, tf), lambda i, j: (0, j)),
        pl.BlockSpec((tf, D_MODEL), lambda i, j: (j, 0)),
    ]
    args = [x, g, wg, wu, wd]
    if final:
        in_specs.append(_const_spec((1, D_MODEL)))
        args.append(gf)
    return pl.pallas_call(
        functools.partial(_ffn_kernel, final=final),
        out_shape=jax.ShapeDtypeStruct((SEQ, D_MODEL), F32),
        grid=(SEQ // tm, D_FF // tf),
        in_specs=in_specs,
        out_specs=pl.BlockSpec((tm, D_MODEL), lambda i, j: (i, 0)),
        scratch_shapes=[pltpu.VMEM((tm, D_MODEL), BF16)],
        compiler_params=_params("arbitrary", "arbitrary"),
        name="ffn_final" if final else "ffn",
    )(*args)


def _mem_kv_kernel(mem_ref, g_ref, wk_ref, wv_ref, k_ref, v_ref):
    mn = _rms(mem_ref[...], g_ref[...]).astype(BF16)
    k_ref[...] = _dot(mn, wk_ref[...]).astype(BF16)
    v_ref[...] = _dot(mn, wv_ref[...]).astype(BF16)


def _mem_kv(mem, g, wk, wv):
    return pl.pallas_call(
        _mem_kv_kernel,
        out_shape=[jax.ShapeDtypeStruct((N_MEM, C_GRP), BF16)] * 2,
        compiler_params=pltpu.CompilerParams(vmem_limit_bytes=VMEM_LIMIT),
        name="mem_kv",
    )(mem, g, wk, wv)


def _mix_in_kernel(x_ref, g_ref, w_ref, wf_ref, fb_ref, poolw_ref, pools_ref,
                   dww_ref, dwb_ref, lng_ref, lnb_ref, pw_ref, scw_ref, gn_ref,
                   y_ref, qa_ref, ka_ref, vt_ref,
                   pbuf, cbuf, sbuf, abuf, carry_ref):
    i = pl.program_id(0)
    tm = x_ref.shape[0]

    @pl.when(i == 0)
    def _():
        pbuf[0:HALO, :] = jnp.zeros((HALO, C_GRP), F32)
        cbuf[0:HALO, :] = jnp.zeros((HALO, C_GRP), F32)
        sbuf[0:HALO, :] = jnp.zeros((HALO, C_GRP), F32)
        carry_ref[...] = jnp.zeros_like(carry_ref)

    h = _rms(x_ref[...], g_ref[...]).astype(BF16)

    def proj(s):
        return _dot(h, w_ref[:, s * C_GRP:(s + 1) * C_GRP])

    u = proj(0)
    glu = proj(1) * jax.nn.sigmoid(proj(2))
    pbuf[HALO:HALO + tm, :] = u
    cbuf[HALO:HALO + tm, :] = glu

    pos = i * tm + lax.broadcasted_iota(jnp.int32, (tm, 1), 0)
    pooled = []
    for gi, w in enumerate(POOL_WINDOWS):
        sl = slice(gi * POOL_GROUP, (gi + 1) * POOL_GROUP)
        tot = u[:, sl]
        for d in range(1, w):
            tot = tot + pbuf[HALO - d:HALO - d + tm, sl]
        count = jnp.minimum(pos + 1, w).astype(F32)
        pooled.append((tot / count - u[:, sl]).astype(BF16))
    pbuf[0:HALO, :] = pbuf[tm:tm + HALO, :]

    first_off = HALO - (CONV_K - 1)

    def conv_taps(r, acc):
        offs = [o for o in range(first_off, first_off + CONV_K) if o % SUBLANES == r]
        base = offs[0]
        if r == 0:
            rows, shift = cbuf, base
        else:
            rows, shift = abuf.at[r - 1], 0
            n = offs[-1] + tm - base
            rows[0:n, :] = cbuf[base:base + n, :]
        for o in offs:
            lo = o - base + shift
            term = dww_ref[o - first_off:o - first_off + 1, :] * rows[lo:lo + tm, :]
            acc = term if acc is None else acc + term
        return acc

    conv = conv_taps(0, None)
    qs = proj(3) * (ATTN_SCALE * LOG2E)
    conv = conv_taps(1, conv)
    ks = proj(4)
    conv = conv_taps(2, conv)
    vt_ref[...] = proj(5).T.astype(BF16)
    conv = conv_taps(3, conv)
    f = _dot(h, wf_ref[...]) + fb_ref[...]
    sb = proj(6)
    conv = conv_taps(4, conv)
    sc = proj(7)
    conv = conv_taps(5, conv)
    sv = sc * proj(8)
    conv = conv_taps(6, conv)
    conv = conv_taps(7, conv)
    cbuf[0:HALO, :] = cbuf[tm:tm + HALO, :]
    conv = conv + dwb_ref[...]
    mu = jnp.mean(conv, axis=-1, keepdims=True)
    xc = conv - mu
    var = jnp.mean(xc * xc, axis=-1, keepdims=True)
    ln = xc * lax.rsqrt(var + EPS) * lng_ref[...] + lnb_ref[...]
    act = (ln * jax.nn.sigmoid(ln)).astype(BF16)
    yb = _dot(act, pw_ref[...])
    ya = jnp.concatenate([_dot(p, poolw_ref[gi]) for gi, p in enumerate(pooled)], axis=-1) * pools_ref[...]
    y_ref[:, 0:C_GRP] = _rms(ya, gn_ref[:, 0:C_GRP]).astype(BF16)
    y_ref[:, C_GRP:2 * C_GRP] = _rms(yb, gn_ref[:, C_GRP:2 * C_GRP]).astype(BF16)

    logf = jnp.minimum(f, 0.0) - jnp.log(1.0 + jnp.exp(-jnp.abs(f)))
    tri = (lax.broadcasted_iota(jnp.int32, (tm, tm), 0)
           >= lax.broadcasted_iota(jnp.int32, (tm, tm), 1)).astype(F32)
    c = jnp.dot(tri, logf, precision=lax.Precision.HIGHEST,
                preferred_element_type=F32) + carry_ref[...]
    carry_ref[...] = c[tm - 1:tm, :]
    c2 = c * LOG2E
    lane = lax.broadcasted_iota(jnp.int32, (tm, LANES), 1)
    ones = jnp.where(lane < 2 * N_SPLIT, 1.0, 0.0)
    for hd in range(HEADS):
        ch = c2[:, hd:hd + 1]
        p1 = ch.astype(BF16).astype(F32)
        r1 = ch - p1
        p2 = r1.astype(BF16).astype(F32)
        p3 = r1 - p2
        qx = jnp.where(lane == 0, p1, jnp.where(lane == 1, p2, jnp.where(lane == 2, p3, ones)))
        kx = jnp.where(lane == 3, -p1, jnp.where(lane == 4, -p2, jnp.where(lane == 5, -p3, ones)))
        sl = slice(hd * HEAD_DIM, (hd + 1) * HEAD_DIM)
        base = hd * 2 * HEAD_DIM
        qa_ref[:, base:base + HEAD_DIM] = qs[:, sl].astype(BF16)
        qa_ref[:, base + HEAD_DIM:base + 2 * HEAD_DIM] = qx.astype(BF16)
        ka_ref[:, base:base + HEAD_DIM] = ks[:, sl].astype(BF16)
        ka_ref[:, base + HEAD_DIM:base + 2 * HEAD_DIM] = kx.astype(BF16)

    sbuf[HALO:HALO + tm, :] = sv
    sconv = scw_ref[SC_K - 1:SC_K, :] * sv
    for kk in range(SC_K - 1):
        off = HALO - (SC_K - 1) + kk
        sconv = sconv + scw_ref[kk:kk + 1, :] * sbuf[off:off + tm, :]
    yd = sb * sconv
    y_ref[:, 2 * C_GRP:3 * C_GRP] = _rms(yd, gn_ref[:, 3 * C_GRP:4 * C_GRP]).astype(BF16)
    sbuf[0:HALO, :] = sbuf[tm:tm + HALO, :]


def _mix_in(x, g, w_main, w_f, f_bias, pool_w, pool_scale, dw_w, dw_b, ln_g, ln_b, pw, sc_w, gn):
    tm = MIX_TM
    row = lambda i: (i, 0)
    n_sec = w_main.shape[1] // C_GRP
    out_shape = [
        jax.ShapeDtypeStruct((SEQ, 3 * C_GRP), BF16),
        jax.ShapeDtypeStruct((SEQ, 2 * C_GRP), BF16),
        jax.ShapeDtypeStruct((SEQ, 2 * C_GRP), BF16),
        jax.ShapeDtypeStruct((C_GRP, SEQ), BF16),
    ]
    out_specs = [
        pl.BlockSpec((tm, 3 * C_GRP), row),
        pl.BlockSpec((tm, 2 * C_GRP), row),
        pl.BlockSpec((tm, 2 * C_GRP), row),
        pl.BlockSpec((C_GRP, tm), lambda i: (0, i)),
    ]
    in_specs = [
        pl.BlockSpec((tm, D_MODEL), row),
        _const_spec((1, D_MODEL)),
        _const_spec((D_MODEL, n_sec * C_GRP)),
        _const_spec((D_MODEL, LANES)),
        _const_spec((1, LANES)),
        _const_spec((len(POOL_WINDOWS), POOL_GROUP, POOL_GROUP)),
        _const_spec((1, C_GRP)),
        _const_spec((CONV_K, C_GRP)),
        _const_spec((1, C_GRP)),
        _const_spec((1, C_GRP)),
        _const_spec((1, C_GRP)),
        _const_spec((C_GRP, C_GRP)),
        _const_spec((SC_K, C_GRP)),
        _const_spec((1, 4 * C_GRP)),
    ]
    return pl.pallas_call(
        _mix_in_kernel,
        out_shape=out_shape,
        grid=(SEQ // tm,),
        in_specs=in_specs,
        out_specs=out_specs,
        scratch_shapes=[pltpu.VMEM((HALO + tm, C_GRP), F32)] * 3 + [
            pltpu.VMEM((SUBLANES - 1, HALO + tm, C_GRP), F32),
            pltpu.VMEM((1, LANES), F32)],
        compiler_params=_params("arbitrary"),
        name="mix_in",
    )(x, g, w_main, w_f, f_bias, pool_w, pool_scale, dw_w, dw_b, ln_g, ln_b, pw, sc_w, gn)


def _fox_kernel(qi_ref, ki_ref, qa_ref, ka_ref, vt_ref, g_ref, o_ref, m_ref, acc_ref):
    step = pl.program_id(0)
    qi = qi_ref[step]
    ki = ki_ref[step]
    t = qa_ref.shape[0]

    @pl.when(ki == 0)
    def _():
        m_ref[...] = jnp.full_like(m_ref, NEG_BIG)
        acc_ref[...] = jnp.zeros_like(acc_ref)

    def accumulate(diagonal):
        ones_rows = (lax.broadcasted_iota(jnp.int32, (ACC_ROWS - HEAD_DIM, t), 0) == 0).astype(BF16)
        if diagonal:
            keep = (lax.broadcasted_iota(jnp.int32, (t, t), 0)
                    <= lax.broadcasted_iota(jnp.int32, (t, t), 1))
        sts, pts, alphas = {}, {}, {}

        def scores(hd):
            wide = slice(hd * 2 * HEAD_DIM, (hd + 1) * 2 * HEAD_DIM)
            st = _dot_nt(ka_ref[:, wide], qa_ref[:, wide])
            sts[hd] = jnp.where(keep, st, NEG_BIG) if diagonal else st

        def softmax(hd):
            m_old = m_ref[hd]
            m_new = jnp.maximum(m_old, jnp.max(sts[hd], axis=0, keepdims=True))
            alphas[hd] = jnp.exp2(m_old - m_new)
            pts[hd] = jnp.exp2(sts[hd] - m_new).astype(BF16)
            m_ref[hd] = m_new

        def values(hd):
            vat = jnp.concatenate([vt_ref[hd * HEAD_DIM:(hd + 1) * HEAD_DIM, :], ones_rows], axis=0)
            acc_ref[hd] = alphas[hd] * acc_ref[hd] + _dot(vat, pts[hd])

        for hd in range(HEADS + 2):
            if hd < HEADS:
                scores(hd)
            if 0 <= hd - 1 < HEADS:
                softmax(hd - 1)
            if 0 <= hd - 2 < HEADS:
                values(hd - 2)

    @pl.when(ki < qi)
    def _():
        accumulate(diagonal=False)

    @pl.when(ki == qi)
    def _():
        accumulate(diagonal=True)
        ot = jnp.concatenate(
            [acc_ref[hd, 0:HEAD_DIM, :] / acc_ref[hd, HEAD_DIM:HEAD_DIM + 1, :] for hd in range(HEADS)],
            axis=0)
        ms = jnp.mean(ot * ot, axis=0, keepdims=True)
        o_ref[...] = (ot * lax.rsqrt(ms + EPS) * g_ref[...]).T.astype(BF16)


def _fox(qa, ka, vt, g):
    t = ATT_T
    n = SEQ // t
    pairs = [(i, j) for i in range(n) for j in range(i + 1)]
    qi = jnp.array([p[0] for p in pairs], jnp.int32)
    ki = jnp.array([p[1] for p in pairs], jnp.int32)
    qmap = lambda s, qi, ki: (qi[s], 0)
    kmap = lambda s, qi, ki: (ki[s], 0)
    return pl.pallas_call(
        _fox_kernel,
        out_shape=jax.ShapeDtypeStruct((SEQ, C_GRP), BF16),
        grid_spec=pltpu.PrefetchScalarGridSpec(
            num_scalar_prefetch=2,
            grid=(len(pairs),),
            in_specs=[
                pl.BlockSpec((t, 2 * C_GRP), qmap),
                pl.BlockSpec((t, 2 * C_GRP), kmap),
                pl.BlockSpec((C_GRP, t), lambda s, qi, ki: (0, ki[s])),
                pl.BlockSpec((C_GRP, 1), lambda s, qi, ki: (0, 0)),
            ],
            out_specs=pl.BlockSpec((t, C_GRP), qmap),
            scratch_shapes=[
                pltpu.VMEM((HEADS, 1, t), F32),
                pltpu.VMEM((HEADS, ACC_ROWS, t), F32),
            ],
        ),
        compiler_params=_params("arbitrary"),
        name="fox_attn",
    )(qi, ki, qa, ka, vt, g)


def _mix_out_kernel(x_ref, yabd_ref, yc_ref, wo_ref, g_ref, wq_ref, km_ref, vm_ref, wxo_ref, o_ref):
    x2 = x_ref[...]
    x2 = x2 + _dot(yabd_ref[:, 0:C_GRP], wo_ref[0:C_GRP, :])
    x2 = x2 + _dot(yabd_ref[:, C_GRP:2 * C_GRP], wo_ref[C_GRP:2 * C_GRP, :])
    x2 = x2 + _dot(yc_ref[...], wo_ref[2 * C_GRP:3 * C_GRP, :])
    x2 = x2 + _dot(yabd_ref[:, 2 * C_GRP:3 * C_GRP], wo_ref[3 * C_GRP:4 * C_GRP, :])

    h = _rms(x2, g_ref[...]).astype(BF16)
    q = (_dot(h, wq_ref[...]) * ATTN_SCALE).astype(BF16)
    outs = []
    for hd in range(HEADS):
        sl = slice(hd * HEAD_DIM, (hd + 1) * HEAD_DIM)
        s = _dot_nt(q[:, sl], km_ref[:, sl])
        e = jnp.exp(s - jnp.max(s, axis=-1, keepdims=True))
        p = (e / jnp.sum(e, axis=-1, keepdims=True)).astype(BF16)
        outs.append(_dot(p, vm_ref[:, sl]))
    o = jnp.concatenate(outs, axis=-1).astype(BF16)
    o_ref[...] = x2 + _dot(o, wxo_ref[...])


def _mix_out(x, yabd, yc, w_out, g, wq, km, vm, wxo):
    tm = OUT_TM
    row = lambda i: (i, 0)
    return pl.pallas_call(
        _mix_out_kernel,
        out_shape=jax.ShapeDtypeStruct((SEQ, D_MODEL), F32),
        grid=(SEQ // tm,),
        in_specs=[
            pl.BlockSpec((tm, D_MODEL), row),
            pl.BlockSpec((tm, 3 * C_GRP), row),
            pl.BlockSpec((tm, C_GRP), row),
            _const_spec((D_MODEL, D_MODEL)),
            _const_spec((1, D_MODEL)),
            _const_spec((D_MODEL, C_GRP)),
            _const_spec((N_MEM, C_GRP)),
            _const_spec((N_MEM, C_GRP)),
            _const_spec((C_GRP, D_MODEL)),
        ],
        out_specs=pl.BlockSpec((tm, D_MODEL), row),
        compiler_params=_params("arbitrary"),
        name="mix_out_xattn",
    )(x, yabd, yc, w_out, g, wq, km, vm, wxo)


def kernel(x, mem, ffn1_norm, ffn1_w_gate, ffn1_w_up, ffn1_w_down, mix_norm, w_mix_in, pool_w, pool_scale, conv_dw_w, conv_dw_b, conv_ln_g, conv_ln_b, conv_pw, fox_f_bias, sc_w, grp_norm, w_mix_out, xa_norm, mem_norm, w_xq, w_xk, w_xv, w_xo, ffn2_norm, ffn2_w_gate, ffn2_w_up, ffn2_w_down, final_norm):
    xs = x.reshape(SEQ, D_MODEL)
    mem2 = mem.reshape(N_MEM, D_MODEL)
    bf = lambda a: a.astype(BF16)
    vec = lambda a: a.reshape(1, -1)
    f_off = 6 * C_GRP
    for l in range(DEPTH):
        xs = _ffn(xs, vec(ffn1_norm[l]), bf(ffn1_w_gate[l]), bf(ffn1_w_up[l]), bf(ffn1_w_down[l]))

        w_in = w_mix_in[l]
        w_main = bf(jnp.concatenate([w_in[:, :f_off], w_in[:, f_off + HEADS:]], axis=1))
        w_f = bf(jnp.pad(w_in[:, f_off:f_off + HEADS], ((0, 0), (0, LANES - HEADS))))
        f_bias = jnp.pad(fox_f_bias[l], (0, LANES - HEADS)).reshape(1, LANES)
        yabd, qa, ka, vt = _mix_in(
            xs, vec(mix_norm[l]), w_main, w_f, f_bias, bf(pool_w[l]), vec(pool_scale[l]),
            conv_dw_w[l], vec(conv_dw_b[l]), vec(conv_ln_g[l]), vec(conv_ln_b[l]),
            bf(conv_pw[l]), sc_w[l], vec(grp_norm[l]))
        yc = _fox(qa, ka, vt, grp_norm[l, 2 * C_GRP:3 * C_GRP].reshape(C_GRP, 1))

        km, vm = _mem_kv(mem2, vec(mem_norm[l]), bf(w_xk[l]), bf(w_xv[l]))
        xs = _mix_out(xs, yabd, yc, bf(w_mix_out[l]), vec(xa_norm[l]), bf(w_xq[l]), km, vm, bf(w_xo[l]))

        gf = vec(final_norm) if l == DEPTH - 1 else None
        xs = _ffn(xs, vec(ffn2_norm[l]), bf(ffn2_w_gate[l]), bf(ffn2_w_up[l]), bf(ffn2_w_down[l]), gf)
    return xs.reshape(1, SEQ, D_MODEL)
```

```python
import functools

import jax
import jax.numpy as jnp
from jax import lax
from jax.experimental import pallas as pl
from jax.experimental.pallas import tpu as pltpu

F32 = jnp.float32
BF16 = jnp.bfloat16

D_MODEL = 2048
SEQ = 16384
DEPTH = 2
C_GRP = 512
POOL_WINDOWS = (2, 4, 8, 16)
POOL_GROUP = 128
CONV_K = 31
SC_K = 3
HEADS = 4
HEAD_DIM = 128
N_MEM = 256
D_FF = 5632
EPS = 1e-6
ATTN_SCALE = HEAD_DIM ** -0.5
LOG2E = 1.4426950408889634
N_SPLIT = 3
ACC_ROWS = HEAD_DIM + 16
LANES = 128
SUBLANES = 8
HALO = 32
NEG_BIG = -1e30
VMEM_LIMIT = 56 * 1024 * 1024

FFN_TM, FFN_TF = 1024, 256
MIX_TM = 256
ATT_TQ, ATT_TK = 1024, 512
OUT_TM = 256


def _rms(x, g):
    ms = jnp.mean(x * x, axis=-1, keepdims=True)
    return x * lax.rsqrt(ms + EPS) * g


def _dot(a, b):
    return jnp.dot(a, b, preferred_element_type=F32)


def _dot_nt(a, b):
    return lax.dot_general(a, b, (((1,), (1,)), ((), ())), preferred_element_type=F32)


def _const_spec(shape):
    return pl.BlockSpec(shape, lambda *_: (0,) * len(shape), pipeline_mode=pl.Buffered(1))


def _params(*sem):
    return pltpu.CompilerParams(dimension_semantics=sem, vmem_limit_bytes=VMEM_LIMIT)


def _ffn_kernel(*refs, final):
    if final:
        x_ref, g_ref, wg_ref, wu_ref, wd_ref, gf_ref, o_ref, h_ref = refs
    else:
        x_ref, g_ref, wg_ref, wu_ref, wd_ref, o_ref, h_ref = refs
    j = pl.program_id(1)

    @pl.when(j == 0)
    def _():
        x = x_ref[...]
        h_ref[...] = _rms(x, g_ref[...]).astype(BF16)
        o_ref[...] = x

    h = h_ref[...]
    gate = _dot(h, wg_ref[...].astype(BF16))
    up = _dot(h, wu_ref[...].astype(BF16))
    a = (0.5 * gate * jax.nn.sigmoid(gate) * up).astype(BF16)
    o_ref[...] += _dot(a, wd_ref[...].astype(BF16))

    if final:
        @pl.when(j == pl.num_programs(1) - 1)
        def _():
            o_ref[...] = _rms(o_ref[...], gf_ref[...])


def _ffn(x, g, wg, wu, wd, layer, gf=None):
    final = gf is not None
    tm, tf = FFN_TM, FFN_TF
    in_specs = [
        pl.BlockSpec((tm, D_MODEL), lambda i, j: (i, 0)),
        _const_spec((1, D_MODEL)),
        pl.BlockSpec((None, D_MODEL, tf), lambda i, j: (layer, 0, j)),
        pl.BlockSpec((None, D_MODEL, tf), lambda i, j: (layer, 0, j)),
        pl.BlockSpec((None, tf, D_MODEL), lambda i, j: (layer, j, 0)),
    ]
    args = [x, g, wg, wu, wd]
    if final:
        in_specs.append(_const_spec((1, D_MODEL)))
        args.append(gf)
    return pl.pallas_call(
        functools.partial(_ffn_kernel, final=final),
        out_shape=jax.ShapeDtypeStruct((SEQ, D_MODEL), F32),
        grid=(SEQ // tm, D_FF // tf),
        in_specs=in_specs,
        out_specs=pl.BlockSpec((tm, D_MODEL), lambda i, j: (i, 0)),
        scratch_shapes=[pltpu.VMEM((tm, D_MODEL), BF16)],
        compiler_params=_params("arbitrary", "arbitrary"),
        name="ffn_final" if final else "ffn",
    )(*args)


def _mem_kv_kernel(mem_ref, g_ref, wk_ref, wv_ref, k_ref, v_ref):
    mn = _rms(mem_ref[...], g_ref[...]).astype(BF16)
    k_ref[...] = _dot(mn, wk_ref[...]).astype(BF16)
    v_ref[...] = _dot(mn, wv_ref[...]).astype(BF16)


def _mem_kv(mem, g, wk, wv):
    return pl.pallas_call(
        _mem_kv_kernel,
        out_shape=[jax.ShapeDtypeStruct((N_MEM, C_GRP), BF16)] * 2,
        compiler_params=pltpu.CompilerParams(vmem_limit_bytes=VMEM_LIMIT),
        name="mem_kv",
    )(mem, g, wk, wv)


def _mix_in_kernel(x_ref, g_ref, w_ref, wf_ref, fb_ref, poolw_ref, pools_ref,
                   dww_ref, dwb_ref, lng_ref, lnb_ref, pw_ref, scw_ref, gn_ref,
                   y_ref, qa_ref, ka_ref, vt_ref,
                   pbuf, cbuf, sbuf, abuf, carry_ref):
    i = pl.program_id(0)
    tm = x_ref.shape[0]

    @pl.when(i == 0)
    def _():
        pbuf[0:HALO, :] = jnp.zeros((HALO, C_GRP), F32)
        cbuf[0:HALO, :] = jnp.zeros((HALO, C_GRP), F32)
        sbuf[0:HALO, :] = jnp.zeros((HALO, C_GRP), F32)
        carry_ref[...] = jnp.zeros_like(carry_ref)

    h = _rms(x_ref[...], g_ref[...]).astype(BF16)

    def proj(s):
        return _dot(h, w_ref[:, s * C_GRP:(s + 1) * C_GRP])

    u = proj(0)
    glu = proj(1) * jax.nn.sigmoid(proj(2))
    pbuf[HALO:HALO + tm, :] = u
    cbuf[HALO:HALO + tm, :] = glu

    pos = i * tm + lax.broadcasted_iota(jnp.int32, (tm, 1), 0)
    pooled = []
    for gi, w in enumerate(POOL_WINDOWS):
        sl = slice(gi * POOL_GROUP, (gi + 1) * POOL_GROUP)
        tot = u[:, sl]
        for d in range(1, w):
            tot = tot + pbuf[HALO - d:HALO - d + tm, sl]
        count = jnp.minimum(pos + 1, w).astype(F32)
        pooled.append((tot / count - u[:, sl]).astype(BF16))
    pbuf[0:HALO, :] = pbuf[tm:tm + HALO, :]

    first_off = HALO - (CONV_K - 1)

    def conv_taps(r, acc):
        offs = [o for o in range(first_off, first_off + CONV_K) if o % SUBLANES == r]
        base = offs[0]
        if r == 0:
            rows, shift = cbuf, base
        else:
            rows, shift = abuf.at[r - 1], 0
            n = offs[-1] + tm - base
            rows[0:n, :] = cbuf[base:base + n, :]
        for o in offs:
            lo = o - base + shift
            term = dww_ref[o - first_off:o - first_off + 1, :] * rows[lo:lo + tm, :]
            acc = term if acc is None else acc + term
        return acc

    conv = conv_taps(0, None)
    qs = proj(3) * (ATTN_SCALE * LOG2E)
    conv = conv_taps(1, conv)
    ks = proj(4)
    conv = conv_taps(2, conv)
    vt_ref[...] = proj(5).T.astype(BF16)
    conv = conv_taps(3, conv)
    f = _dot(h, wf_ref[...]) + fb_ref[...]
    sb = proj(6)
    conv = conv_taps(4, conv)
    sc = proj(7)
    conv = conv_taps(5, conv)
    sv = sc * proj(8)
    conv = conv_taps(6, conv)
    conv = conv_taps(7, conv)
    cbuf[0:HALO, :] = cbuf[tm:tm + HALO, :]
    conv = conv + dwb_ref[...]
    mu = jnp.mean(conv, axis=-1, keepdims=True)
    xc = conv - mu
    var = jnp.mean(xc * xc, axis=-1, keepdims=True)
    ln = xc * lax.rsqrt(var + EPS) * lng_ref[...] + lnb_ref[...]
    act = (ln * jax.nn.sigmoid(ln)).astype(BF16)
    yb = _dot(act, pw_ref[...])
    ya = jnp.concatenate([_dot(p, poolw_ref[gi]) for gi, p in enumerate(pooled)], axis=-1) * pools_ref[...]
    y_ref[:, 0:C_GRP] = _rms(ya, gn_ref[:, 0:C_GRP]).astype(BF16)
    y_ref[:, C_GRP:2 * C_GRP] = _rms(yb, gn_ref[:, C_GRP:2 * C_GRP]).astype(BF16)

    logf = jnp.minimum(f, 0.0) - jnp.log(1.0 + jnp.exp(-jnp.abs(f)))
    tri = (lax.broadcasted_iota(jnp.int32, (tm, tm), 0)
           >= lax.broadcasted_iota(jnp.int32, (tm, tm), 1)).astype(F32)
    c = jnp.dot(tri, logf, precision=lax.Precision.HIGHEST,
                preferred_element_type=F32) + carry_ref[...]
    carry_ref[...] = c[tm - 1:tm, :]
    c2 = c * LOG2E
    lane = lax.broadcasted_iota(jnp.int32, (tm, LANES), 1)
    ones = jnp.where(lane < 2 * N_SPLIT, 1.0, 0.0)
    for hd in range(HEADS):
        ch = c2[:, hd:hd + 1]
        p1 = ch.astype(BF16).astype(F32)
        r1 = ch - p1
        p2 = r1.astype(BF16).astype(F32)
        p3 = r1 - p2
        qx = jnp.where(lane == 0, p1, jnp.where(lane == 1, p2, jnp.where(lane == 2, p3, ones)))
        kx = jnp.where(lane == 3, -p1, jnp.where(lane == 4, -p2, jnp.where(lane == 5, -p3, ones)))
        sl = slice(hd * HEAD_DIM, (hd + 1) * HEAD_DIM)
        base = hd * 2 * HEAD_DIM
        qa_ref[:, base:base + HEAD_DIM] = qs[:, sl].astype(BF16)
        qa_ref[:, base + HEAD_DIM:base + 2 * HEAD_DIM] = qx.astype(BF16)
        ka_ref[:, base:base + HEAD_DIM] = ks[:, sl].astype(BF16)
        ka_ref[:, base + HEAD_DIM:base + 2 * HEAD_DIM] = kx.astype(BF16)

    sbuf[HALO:HALO + tm, :] = sv
    sconv = scw_ref[SC_K - 1:SC_K, :] * sv
    for kk in range(SC_K - 1):
        off = HALO - (SC_K - 1) + kk
        sconv = sconv + scw_ref[kk:kk + 1, :] * sbuf[off:off + tm, :]
    yd = sb * sconv
    y_ref[:, 2 * C_GRP:3 * C_GRP] = _rms(yd, gn_ref[:, 3 * C_GRP:4 * C_GRP]).astype(BF16)
    sbuf[0:HALO, :] = sbuf[tm:tm + HALO, :]


def _mix_in(x, g, w_main, w_f, f_bias, pool_w, pool_scale, dw_w, dw_b, ln_g, ln_b, pw, sc_w, gn):
    tm = MIX_TM
    row = lambda i: (i, 0)
    n_sec = w_main.shape[1] // C_GRP
    out_shape = [
        jax.ShapeDtypeStruct((SEQ, 3 * C_GRP), BF16),
        jax.ShapeDtypeStruct((SEQ, 2 * C_GRP), BF16),
        jax.ShapeDtypeStruct((SEQ, 2 * C_GRP), BF16),
        jax.ShapeDtypeStruct((C_GRP, SEQ), BF16),
    ]
    out_specs = [
        pl.BlockSpec((tm, 3 * C_GRP), row),
        pl.BlockSpec((tm, 2 * C_GRP), row),
        pl.BlockSpec((tm, 2 * C_GRP), row),
        pl.BlockSpec((C_GRP, tm), lambda i: (0, i)),
    ]
    in_specs = [
        pl.BlockSpec((tm, D_MODEL), row),
        _const_spec((1, D_MODEL)),
        _const_spec((D_MODEL, n_sec * C_GRP)),
        _const_spec((D_MODEL, LANES)),
        _const_spec((1, LANES)),
        _const_spec((len(POOL_WINDOWS), POOL_GROUP, POOL_GROUP)),
        _const_spec((1, C_GRP)),
        _const_spec((CONV_K, C_GRP)),
        _const_spec((1, C_GRP)),
        _const_spec((1, C_GRP)),
        _const_spec((1, C_GRP)),
        _const_spec((C_GRP, C_GRP)),
        _const_spec((SC_K, C_GRP)),
        _const_spec((1, 4 * C_GRP)),
    ]
    return pl.pallas_call(
        _mix_in_kernel,
        out_shape=out_shape,
        grid=(SEQ // tm,),
        in_specs=in_specs,
        out_specs=out_specs,
        scratch_shapes=[pltpu.VMEM((HALO + tm, C_GRP), F32)] * 3 + [
            pltpu.VMEM((SUBLANES - 1, HALO + tm, C_GRP), F32),
            pltpu.VMEM((1, LANES), F32)],
        compiler_params=_params("arbitrary"),
        name="mix_in",
    )(x, g, w_main, w_f, f_bias, pool_w, pool_scale, dw_w, dw_b, ln_g, ln_b, pw, sc_w, gn)


def _fox_kernel(qi_ref, ki_ref, qa_ref, ka_ref, vt_ref, g_ref, o_ref, m_ref, acc_ref):
    step = pl.program_id(0)
    qi = qi_ref[step]
    ki = ki_ref[step]
    tq = qa_ref.shape[0]
    tk = ka_ref.shape[0]
    first_diag = qi * (tq // tk)
    last = first_diag + tq // tk - 1

    @pl.when(ki == 0)
    def _():
        m_ref[...] = jnp.full_like(m_ref, NEG_BIG)
        acc_ref[...] = jnp.zeros_like(acc_ref)

    def accumulate(diagonal):
        ones_rows = (lax.broadcasted_iota(jnp.int32, (ACC_ROWS - HEAD_DIM, tk), 0) == 0).astype(BF16)
        if diagonal:
            keep = (lax.broadcasted_iota(jnp.int32, (tk, tq), 0) + (ki - first_diag) * tk
                    <= lax.broadcasted_iota(jnp.int32, (tk, tq), 1))
        sts, pts, alphas = {}, {}, {}

        def scores(hd):
            wide = slice(hd * 2 * HEAD_DIM, (hd + 1) * 2 * HEAD_DIM)
            st = _dot_nt(ka_ref[:, wide], qa_ref[:, wide])
            sts[hd] = jnp.where(keep, st, NEG_BIG) if diagonal else st

        def softmax(hd):
            m_old = m_ref[hd]
            m_new = jnp.maximum(m_old, jnp.max(sts[hd], axis=0, keepdims=True))
            alphas[hd] = jnp.exp2(m_old - m_new)
            pts[hd] = jnp.exp2(sts[hd] - m_new).astype(BF16)
            m_ref[hd] = m_new

        def values(hd):
            vat = jnp.concatenate([vt_ref[hd * HEAD_DIM:(hd + 1) * HEAD_DIM, :], ones_rows], axis=0)
            acc_ref[hd] = alphas[hd] * acc_ref[hd] + _dot(vat, pts[hd])

        for hd in range(HEADS + 2):
            if hd < HEADS:
                scores(hd)
            if 0 <= hd - 1 < HEADS:
                softmax(hd - 1)
            if 0 <= hd - 2 < HEADS:
                values(hd - 2)

    @pl.when(ki < first_diag)
    def _():
        accumulate(diagonal=False)

    @pl.when(ki >= first_diag)
    def _():
        accumulate(diagonal=True)

    @pl.when(ki == last)
    def _():
        ot = jnp.concatenate(
            [acc_ref[hd, 0:HEAD_DIM, :] / acc_ref[hd, HEAD_DIM:HEAD_DIM + 1, :] for hd in range(HEADS)],
            axis=0)
        ms = jnp.mean(ot * ot, axis=0, keepdims=True)
        o_ref[...] = (ot * lax.rsqrt(ms + EPS) * g_ref[...]).T.astype(BF16)


def _fox(qa, ka, vt, g):
    tq, tk = ATT_TQ, ATT_TK
    ratio = tq // tk
    pairs = [(i, j) for i in range(SEQ // tq) for j in range((i + 1) * ratio)]
    qi = jnp.array([p[0] for p in pairs], jnp.int32)
    ki = jnp.array([p[1] for p in pairs], jnp.int32)
    qmap = lambda s, qi, ki: (qi[s], 0)
    kmap = lambda s, qi, ki: (ki[s], 0)
    return pl.pallas_call(
        _fox_kernel,
        out_shape=jax.ShapeDtypeStruct((SEQ, C_GRP), BF16),
        grid_spec=pltpu.PrefetchScalarGridSpec(
            num_scalar_prefetch=2,
            grid=(len(pairs),),
            in_specs=[
                pl.BlockSpec((tq, 2 * C_GRP), qmap),
                pl.BlockSpec((tk, 2 * C_GRP), kmap),
                pl.BlockSpec((C_GRP, tk), lambda s, qi, ki: (0, ki[s])),
                pl.BlockSpec((C_GRP, 1), lambda s, qi, ki: (0, 0)),
            ],
            out_specs=pl.BlockSpec((tq, C_GRP), qmap),
            scratch_shapes=[
                pltpu.VMEM((HEADS, 1, tq), F32),
                pltpu.VMEM((HEADS, ACC_ROWS, tq), F32),
            ],
        ),
        compiler_params=_params("arbitrary"),
        name="fox_attn",
    )(qi, ki, qa, ka, vt, g)


def _mix_out_kernel(x_ref, yabd_ref, yc_ref, wo_ref, g_ref, wq_ref, km_ref, vm_ref, wxo_ref, o_ref):
    x2 = x_ref[...]
    x2 = x2 + _dot(yabd_ref[:, 0:C_GRP], wo_ref[0:C_GRP, :])
    x2 = x2 + _dot(yabd_ref[:, C_GRP:2 * C_GRP], wo_ref[C_GRP:2 * C_GRP, :])
    x2 = x2 + _dot(yc_ref[...], wo_ref[2 * C_GRP:3 * C_GRP, :])
    x2 = x2 + _dot(yabd_ref[:, 2 * C_GRP:3 * C_GRP], wo_ref[3 * C_GRP:4 * C_GRP, :])

    h = _rms(x2, g_ref[...]).astype(BF16)
    q = (_dot(h, wq_ref[...]) * ATTN_SCALE).astype(BF16)
    outs = []
    for hd in range(HEADS):
        sl = slice(hd * HEAD_DIM, (hd + 1) * HEAD_DIM)
        s = _dot_nt(q[:, sl], km_ref[:, sl])
        e = jnp.exp(s - jnp.max(s, axis=-1, keepdims=True))
        p = (e / jnp.sum(e, axis=-1, keepdims=True)).astype(BF16)
        outs.append(_dot(p, vm_ref[:, sl]))
    o = jnp.concatenate(outs, axis=-1).astype(BF16)
    o_ref[...] = x2 + _dot(o, wxo_ref[...])


def _mix_out(x, yabd, yc, w_out, g, wq, km, vm, wxo):
    tm = OUT_TM
    row = lambda i: (i, 0)
    return pl.pallas_call(
        _mix_out_kernel,
        out_shape=jax.ShapeDtypeStruct((SEQ, D_MODEL), F32),
        grid=(SEQ // tm,),
        in_specs=[
            pl.BlockSpec((tm, D_MODEL), row),
            pl.BlockSpec((tm, 3 * C_GRP), row),
            pl.BlockSpec((tm, C_GRP), row),
            _const_spec((D_MODEL, D_MODEL)),
            _const_spec((1, D_MODEL)),
            _const_spec((D_MODEL, C_GRP)),
            _const_spec((N_MEM, C_GRP)),
            _const_spec((N_MEM, C_GRP)),
            _const_spec((C_GRP, D_MODEL)),
        ],
        out_specs=pl.BlockSpec((tm, D_MODEL), row),
        compiler_params=_params("arbitrary"),
        name="mix_out_xattn",
    )(x, yabd, yc, w_out, g, wq, km, vm, wxo)


def kernel(x, mem, ffn1_norm, ffn1_w_gate, ffn1_w_up, ffn1_w_down, mix_norm, w_mix_in, pool_w, pool_scale, conv_dw_w, conv_dw_b, conv_ln_g, conv_ln_b, conv_pw, fox_f_bias, sc_w, grp_norm, w_mix_out, xa_norm, mem_norm, w_xq, w_xk, w_xv, w_xo, ffn2_norm, ffn2_w_gate, ffn2_w_up, ffn2_w_down, final_norm):
    xs = x.reshape(SEQ, D_MODEL)
    mem2 = mem.reshape(N_MEM, D_MODEL)
    bf = lambda a: a.astype(BF16)
    vec = lambda a: a.reshape(1, -1)
    f_off = 6 * C_GRP
    for l in range(DEPTH):
        xs = _ffn(xs, vec(ffn1_norm[l]), ffn1_w_gate, ffn1_w_up, ffn1_w_down, l)

        w_in = w_mix_in[l]
        w_main = bf(jnp.concatenate([w_in[:, :f_off], w_in[:, f_off + HEADS:]], axis=1))
        w_f = bf(jnp.pad(w_in[:, f_off:f_off + HEADS], ((0, 0), (0, LANES - HEADS))))
        f_bias = jnp.pad(fox_f_bias[l], (0, LANES - HEADS)).reshape(1, LANES)
        yabd, qa, ka, vt = _mix_in(
            xs, vec(mix_norm[l]), w_main, w_f, f_bias, bf(pool_w[l]), vec(pool_scale[l]),
            conv_dw_w[l], vec(conv_dw_b[l]), vec(conv_ln_g[l]), vec(conv_ln_b[l]),
            bf(conv_pw[l]), sc_w[l], vec(grp_norm[l]))
        yc = _fox(qa, ka, vt, grp_norm[l, 2 * C_GRP:3 * C_GRP].reshape(C_GRP, 1))

        km, vm = _mem_kv(mem2, vec(mem_norm[l]), bf(w_xk[l]), bf(w_xv[l]))
        xs = _mix_out(xs, yabd, yc, bf(w_mix_out[l]), vec(xa_norm[l]), bf(w_xq[l]), km, vm, bf(w_xo[l]))

        gf = vec(final_norm) if l == DEPTH - 1 else None
        xs = _ffn(xs, vec(ffn2_norm[l]), ffn2_w_gate, ffn2_w_up, ffn2_w_down, l, gf)
    return xs.reshape(1, SEQ, D_MODEL)
```

```python
import functools

import jax
import jax.numpy as jnp
from jax import lax
from jax.experimental import pallas as pl
from jax.experimental.pallas import tpu as pltpu

F32 = jnp.float32
BF16 = jnp.bfloat16

D_MODEL = 2048
SEQ = 16384
DEPTH = 2
C_GRP = 512
POOL_WINDOWS = (2, 4, 8, 16)
POOL_GROUP = 128
CONV_K = 31
SC_K = 3
HEADS = 4
HEAD_DIM = 128
N_MEM = 256
D_FF = 5632
EPS = 1e-6
ATTN_SCALE = HEAD_DIM ** -0.5
LOG2E = 1.4426950408889634
N_SPLIT = 3
ACC_ROWS = HEAD_DIM + 16
LANES = 128
SUBLANES = 8
HALO = 32
NEG_BIG = -1e30
VMEM_LIMIT = 56 * 1024 * 1024

CAST_BLOCK_BYTES = 6 * 1024 * 1024
FFN_TM, FFN_TF = 1024, 512
MIX_TM = 256
ATT_TQ, ATT_TK = 1024, 512
OUT_TM = 256


def _rms(x, g):
    ms = jnp.mean(x * x, axis=-1, keepdims=True)
    return x * lax.rsqrt(ms + EPS) * g


def _dot(a, b):
    return jnp.dot(a, b, preferred_element_type=F32)


def _dot_nt(a, b):
    return lax.dot_general(a, b, (((1,), (1,)), ((), ())), preferred_element_type=F32)


def _const_spec(shape):
    return pl.BlockSpec(shape, lambda *_: (0,) * len(shape), pipeline_mode=pl.Buffered(1))


def _params(*sem):
    return pltpu.CompilerParams(dimension_semantics=sem, vmem_limit_bytes=VMEM_LIMIT)


def _ffn_kernel(*refs, final):
    if final:
        x_ref, g_ref, wg_ref, wu_ref, wd_ref, gf_ref, o_ref, h_ref = refs
    else:
        x_ref, g_ref, wg_ref, wu_ref, wd_ref, o_ref, h_ref = refs
    j = pl.program_id(1)

    @pl.when(j == 0)
    def _():
        x = x_ref[...]
        h_ref[...] = _rms(x, g_ref[...]).astype(BF16)
        o_ref[...] = x

    h = h_ref[...]
    gate = _dot(h, wg_ref[...])
    up = _dot(h, wu_ref[...])
    a = (0.5 * gate * jax.nn.sigmoid(gate) * up).astype(BF16)
    o_ref[...] += _dot(a, wd_ref[...])

    if final:
        @pl.when(j == pl.num_programs(1) - 1)
        def _():
            o_ref[...] = _rms(o_ref[...], gf_ref[...])


def _ffn(x, g, wg, wu, wd, gf=None):
    final = gf is not None
    tm, tf = FFN_TM, FFN_TF
    in_specs = [
        pl.BlockSpec((tm, D_MODEL), lambda i, j: (i, 0)),
        _const_spec((1, D_MODEL)),
        pl.BlockSpec((D_MODEL, tf), lambda i, j: (0, j)),
        pl.BlockSpec((D_MODEL, tf), lambda i, j: (0, j)),
        pl.BlockSpec((tf, D_MODEL), lambda i, j: (j, 0)),
    ]
    args = [x, g, wg, wu, wd]
    if final:
        in_specs.append(_const_spec((1, D_MODEL)))
        args.append(gf)
    return pl.pallas_call(
        functools.partial(_ffn_kernel, final=final),
        out_shape=jax.ShapeDtypeStruct((SEQ, D_MODEL), F32),
        grid=(SEQ // tm, D_FF // tf),
        in_specs=in_specs,
        out_specs=pl.BlockSpec((tm, D_MODEL), lambda i, j: (i, 0)),
        scratch_shapes=[pltpu.VMEM((tm, D_MODEL), BF16)],
        compiler_params=_params("arbitrary", "arbitrary"),
        name="ffn_final" if final else "ffn",
    )(*args)


def _round_kernel(w_ref, o_ref):
    o_ref[...] = w_ref[...].astype(BF16)


def _round_bf16(w, layer):
    _, k, n = w.shape
    bk = max(b for b in range(16, k + 1, 16) if k % b == 0 and b * n * 4 <= CAST_BLOCK_BYTES)
    return pl.pallas_call(
        _round_kernel,
        out_shape=jax.ShapeDtypeStruct((k, n), BF16),
        grid=(k // bk,),
        in_specs=[pl.BlockSpec((None, bk, n), lambda i: (layer, i, 0))],
        out_specs=pl.BlockSpec((bk, n), lambda i: (i, 0)),
        compiler_params=_params("arbitrary"),
        name="round_bf16",
    )(w)


def _mem_kv_kernel(mem_ref, g_ref, wk_ref, wv_ref, k_ref, v_ref):
    mn = _rms(mem_ref[...], g_ref[...]).astype(BF16)
    k_ref[...] = _dot(mn, wk_ref[...]).astype(BF16)
    v_ref[...] = _dot(mn, wv_ref[...]).astype(BF16)


def _mem_kv(mem, g, wk, wv):
    return pl.pallas_call(
        _mem_kv_kernel,
        out_shape=[jax.ShapeDtypeStruct((N_MEM, C_GRP), BF16)] * 2,
        compiler_params=pltpu.CompilerParams(vmem_limit_bytes=VMEM_LIMIT),
        name="mem_kv",
    )(mem, g, wk, wv)


def _mix_in_kernel(x_ref, g_ref, w_ref, wf_ref, fb_ref, poolw_ref, pools_ref,
                   dww_ref, dwb_ref, lng_ref, lnb_ref, pw_ref, scw_ref, gn_ref,
                   y_ref, qa_ref, ka_ref, vt_ref,
                   pbuf, cbuf, sbuf, abuf, carry_ref):
    i = pl.program_id(0)
    tm = x_ref.shape[0]

    @pl.when(i == 0)
    def _():
        pbuf[0:HALO, :] = jnp.zeros((HALO, C_GRP), F32)
        cbuf[0:HALO, :] = jnp.zeros((HALO, C_GRP), F32)
        sbuf[0:HALO, :] = jnp.zeros((HALO, C_GRP), F32)
        carry_ref[...] = jnp.zeros_like(carry_ref)

    h = _rms(x_ref[...], g_ref[...]).astype(BF16)

    def proj(s):
        return _dot(h, w_ref[:, s * C_GRP:(s + 1) * C_GRP])

    u = proj(0)
    glu = proj(1) * jax.nn.sigmoid(proj(2))
    pbuf[HALO:HALO + tm, :] = u
    cbuf[HALO:HALO + tm, :] = glu

    pos = i * tm + lax.broadcasted_iota(jnp.int32, (tm, 1), 0)
    pooled = []
    for gi, w in enumerate(POOL_WINDOWS):
        sl = slice(gi * POOL_GROUP, (gi + 1) * POOL_GROUP)
        tot = u[:, sl]
        for d in range(1, w):
            tot = tot + pbuf[HALO - d:HALO - d + tm, sl]
        count = jnp.minimum(pos + 1, w).astype(F32)
        pooled.append((tot / count - u[:, sl]).astype(BF16))
    pbuf[0:HALO, :] = pbuf[tm:tm + HALO, :]

    first_off = HALO - (CONV_K - 1)

    def conv_taps(r, acc):
        offs = [o for o in range(first_off, first_off + CONV_K) if o % SUBLANES == r]
        base = offs[0]
        if r == 0:
            rows, shift = cbuf, base
        else:
            rows, shift = abuf.at[r - 1], 0
            n = offs[-1] + tm - base
            rows[0:n, :] = cbuf[base:base + n, :]
        for o in offs:
            lo = o - base + shift
            term = dww_ref[o - first_off:o - first_off + 1, :] * rows[lo:lo + tm, :]
            acc = term if acc is None else acc + term
        return acc

    conv = conv_taps(0, None)
    qs = proj(3) * (ATTN_SCALE * LOG2E)
    conv = conv_taps(1, conv)
    ks = proj(4)
    conv = conv_taps(2, conv)
    vt_ref[...] = proj(5).T.astype(BF16)
    conv = conv_taps(3, conv)
    f = _dot(h, wf_ref[...]) + fb_ref[...]
    sb = proj(6)
    conv = conv_taps(4, conv)
    sc = proj(7)
    conv = conv_taps(5, conv)
    sv = sc * proj(8)
    conv = conv_taps(6, conv)
    conv = conv_taps(7, conv)
    cbuf[0:HALO, :] = cbuf[tm:tm + HALO, :]
    conv = conv + dwb_ref[...]
    mu = jnp.mean(conv, axis=-1, keepdims=True)
    xc = conv - mu
    var = jnp.mean(xc * xc, axis=-1, keepdims=True)
    ln = xc * lax.rsqrt(var + EPS) * lng_ref[...] + lnb_ref[...]
    act = (ln * jax.nn.sigmoid(ln)).astype(BF16)
    yb = _dot(act, pw_ref[...])
    ya = jnp.concatenate([_dot(p, poolw_ref[gi]) for gi, p in enumerate(pooled)], axis=-1) * pools_ref[...]
    y_ref[:, 0:C_GRP] = _rms(ya, gn_ref[:, 0:C_GRP]).astype(BF16)
    y_ref[:, C_GRP:2 * C_GRP] = _rms(yb, gn_ref[:, C_GRP:2 * C_GRP]).astype(BF16)

    logf = jnp.minimum(f, 0.0) - jnp.log(1.0 + jnp.exp(-jnp.abs(f)))
    tri = (lax.broadcasted_iota(jnp.int32, (tm, tm), 0)
           >= lax.broadcasted_iota(jnp.int32, (tm, tm), 1)).astype(F32)
    c = jnp.dot(tri, logf, precision=lax.Precision.HIGHEST,
                preferred_element_type=F32) + carry_ref[...]
    carry_ref[...] = c[tm - 1:tm, :]
    c2 = c * LOG2E
    lane = lax.broadcasted_iota(jnp.int32, (tm, LANES), 1)
    ones = jnp.where(lane < 2 * N_SPLIT, 1.0, 0.0)
    for hd in range(HEADS):
        ch = c2[:, hd:hd + 1]
        p1 = ch.astype(BF16).astype(F32)
        r1 = ch - p1
        p2 = r1.astype(BF16).astype(F32)
        p3 = r1 - p2
        qx = jnp.where(lane == 0, p1, jnp.where(lane == 1, p2, jnp.where(lane == 2, p3, ones)))
        kx = jnp.where(lane == 3, -p1, jnp.where(lane == 4, -p2, jnp.where(lane == 5, -p3, ones)))
        sl = slice(hd * HEAD_DIM, (hd + 1) * HEAD_DIM)
        base = hd * 2 * HEAD_DIM
        qa_ref[:, base:base + HEAD_DIM] = qs[:, sl].astype(BF16)
        qa_ref[:, base + HEAD_DIM:base + 2 * HEAD_DIM] = qx.astype(BF16)
        ka_ref[:, base:base + HEAD_DIM] = ks[:, sl].astype(BF16)
        ka_ref[:, base + HEAD_DIM:base + 2 * HEAD_DIM] = kx.astype(BF16)

    sbuf[HALO:HALO + tm, :] = sv
    sconv = scw_ref[SC_K - 1:SC_K, :] * sv
    for kk in range(SC_K - 1):
        off = HALO - (SC_K - 1) + kk
        sconv = sconv + scw_ref[kk:kk + 1, :] * sbuf[off:off + tm, :]
    yd = sb * sconv
    y_ref[:, 2 * C_GRP:3 * C_GRP] = _rms(yd, gn_ref[:, 3 * C_GRP:4 * C_GRP]).astype(BF16)
    sbuf[0:HALO, :] = sbuf[tm:tm + HALO, :]


def _mix_in(x, g, w_main, w_f, f_bias, pool_w, pool_scale, dw_w, dw_b, ln_g, ln_b, pw, sc_w, gn):
    tm = MIX_TM
    row = lambda i: (i, 0)
    n_sec = w_main.shape[1] // C_GRP
    out_shape = [
        jax.ShapeDtypeStruct((SEQ, 3 * C_GRP), BF16),
        jax.ShapeDtypeStruct((SEQ, 2 * C_GRP), BF16),
        jax.ShapeDtypeStruct((SEQ, 2 * C_GRP), BF16),
        jax.ShapeDtypeStruct((C_GRP, SEQ), BF16),
    ]
    out_specs = [
        pl.BlockSpec((tm, 3 * C_GRP), row),
        pl.BlockSpec((tm, 2 * C_GRP), row),
        pl.BlockSpec((tm, 2 * C_GRP), row),
        pl.BlockSpec((C_GRP, tm), lambda i: (0, i)),
    ]
    in_specs = [
        pl.BlockSpec((tm, D_MODEL), row),
        _const_spec((1, D_MODEL)),
        _const_spec((D_MODEL, n_sec * C_GRP)),
        _const_spec((D_MODEL, LANES)),
        _const_spec((1, LANES)),
        _const_spec((len(POOL_WINDOWS), POOL_GROUP, POOL_GROUP)),
        _const_spec((1, C_GRP)),
        _const_spec((CONV_K, C_GRP)),
        _const_spec((1, C_GRP)),
        _const_spec((1, C_GRP)),
        _const_spec((1, C_GRP)),
        _const_spec((C_GRP, C_GRP)),
        _const_spec((SC_K, C_GRP)),
        _const_spec((1, 4 * C_GRP)),
    ]
    return pl.pallas_call(
        _mix_in_kernel,
        out_shape=out_shape,
        grid=(SEQ // tm,),
        in_specs=in_specs,
        out_specs=out_specs,
        scratch_shapes=[pltpu.VMEM((HALO + tm, C_GRP), F32)] * 3 + [
            pltpu.VMEM((SUBLANES - 1, HALO + tm, C_GRP), F32),
            pltpu.VMEM((1, LANES), F32)],
        compiler_params=_params("arbitrary"),
        name="mix_in",
    )(x, g, w_main, w_f, f_bias, pool_w, pool_scale, dw_w, dw_b, ln_g, ln_b, pw, sc_w, gn)


def _fox_kernel(qi_ref, ki_ref, qa_ref, ka_ref, vt_ref, g_ref, o_ref, m_ref, acc_ref):
    step = pl.program_id(0)
    qi = qi_ref[step]
    ki = ki_ref[step]
    tq = qa_ref.shape[0]
    tk = ka_ref.shape[0]
    first_diag = qi * (tq // tk)
    last = first_diag + tq // tk - 1

    @pl.when(ki == 0)
    def _():
        m_ref[...] = jnp.full_like(m_ref, NEG_BIG)
        acc_ref[...] = jnp.zeros_like(acc_ref)

    def accumulate(diagonal):
        ones_rows = (lax.broadcasted_iota(jnp.int32, (ACC_ROWS - HEAD_DIM, tk), 0) == 0).astype(BF16)
        if diagonal:
            keep = (lax.broadcasted_iota(jnp.int32, (tk, tq), 0) + (ki - first_diag) * tk
                    <= lax.broadcasted_iota(jnp.int32, (tk, tq), 1))
        sts, pts, alphas = {}, {}, {}

        def scores(hd):
            wide = slice(hd * 2 * HEAD_DIM, (hd + 1) * 2 * HEAD_DIM)
            st = _dot_nt(ka_ref[:, wide], qa_ref[:, wide])
            sts[hd] = jnp.where(keep, st, NEG_BIG) if diagonal else st

        def softmax(hd):
            m_old = m_ref[hd]
            m_new = jnp.maximum(m_old, jnp.max(sts[hd], axis=0, keepdims=True))
            alphas[hd] = jnp.exp2(m_old - m_new)
            pts[hd] = jnp.exp2(sts[hd] - m_new).astype(BF16)
            m_ref[hd] = m_new

        def values(hd):
            vat = jnp.concatenate([vt_ref[hd * HEAD_DIM:(hd + 1) * HEAD_DIM, :], ones_rows], axis=0)
            acc_ref[hd] = alphas[hd] * acc_ref[hd] + _dot(vat, pts[hd])

        for hd in range(HEADS + 2):
            if hd < HEADS:
                scores(hd)
            if 0 <= hd - 1 < HEADS:
                softmax(hd - 1)
            if 0 <= hd - 2 < HEADS:
                values(hd - 2)

    @pl.when(ki < first_diag)
    def _():
        accumulate(diagonal=False)

    @pl.when(ki >= first_diag)
    def _():
        accumulate(diagonal=True)

    @pl.when(ki == last)
    def _():
        ot = jnp.concatenate(
            [acc_ref[hd, 0:HEAD_DIM, :] / acc_ref[hd, HEAD_DIM:HEAD_DIM + 1, :] for hd in range(HEADS)],
            axis=0)
        ms = jnp.mean(ot * ot, axis=0, keepdims=True)
        o_ref[...] = (ot * lax.rsqrt(ms + EPS) * g_ref[...]).T.astype(BF16)


def _fox(qa, ka, vt, g):
    tq, tk = ATT_TQ, ATT_TK
    ratio = tq // tk
    pairs = [(i, j) for i in range(SEQ // tq) for j in range((i + 1) * ratio)]
    qi = jnp.array([p[0] for p in pairs], jnp.int32)
    ki = jnp.array([p[1] for p in pairs], jnp.int32)
    qmap = lambda s, qi, ki: (qi[s], 0)
    kmap = lambda s, qi, ki: (ki[s], 0)
    return pl.pallas_call(
        _fox_kernel,
        out_shape=jax.ShapeDtypeStruct((SEQ, C_GRP), BF16),
        grid_spec=pltpu.PrefetchScalarGridSpec(
            num_scalar_prefetch=2,
            grid=(len(pairs),),
            in_specs=[
                pl.BlockSpec((tq, 2 * C_GRP), qmap),
                pl.BlockSpec((tk, 2 * C_GRP), kmap),
                pl.BlockSpec((C_GRP, tk), lambda s, qi, ki: (0, ki[s])),
                pl.BlockSpec((C_GRP, 1), lambda s, qi, ki: (0, 0)),
            ],
            out_specs=pl.BlockSpec((tq, C_GRP), qmap),
            scratch_shapes=[
                pltpu.VMEM((HEADS, 1, tq), F32),
                pltpu.VMEM((HEADS, ACC_ROWS, tq), F32),
            ],
        ),
        compiler_params=_params("arbitrary"),
        name="fox_attn",
    )(qi, ki, qa, ka, vt, g)


def _mix_out_kernel(x_ref, yabd_ref, yc_ref, wo_ref, g_ref, wq_ref, km_ref, vm_ref, wxo_ref, o_ref):
    y = jnp.concatenate([yabd_ref[:, 0:2 * C_GRP], yc_ref[...], yabd_ref[:, 2 * C_GRP:3 * C_GRP]], axis=1)
    x2 = x_ref[...] + _dot(y, wo_ref[...])

    h = _rms(x2, g_ref[...]).astype(BF16)
    q = (_dot(h, wq_ref[...]) * ATTN_SCALE).astype(BF16)
    outs = []
    for hd in range(HEADS):
        sl = slice(hd * HEAD_DIM, (hd + 1) * HEAD_DIM)
        s = _dot_nt(q[:, sl], km_ref[:, sl])
        e = jnp.exp(s - jnp.max(s, axis=-1, keepdims=True))
        p = (e / jnp.sum(e, axis=-1, keepdims=True)).astype(BF16)
        outs.append(_dot(p, vm_ref[:, sl]))
    o = jnp.concatenate(outs, axis=-1).astype(BF16)
    o_ref[...] = x2 + _dot(o, wxo_ref[...])


def _mix_out(x, yabd, yc, w_out, g, wq, km, vm, wxo):
    tm = OUT_TM
    row = lambda i: (i, 0)
    return pl.pallas_call(
        _mix_out_kernel,
        out_shape=jax.ShapeDtypeStruct((SEQ, D_MODEL), F32),
        grid=(SEQ // tm,),
        in_specs=[
            pl.BlockSpec((tm, D_MODEL), row),
            pl.BlockSpec((tm, 3 * C_GRP), row),
            pl.BlockSpec((tm, C_GRP), row),
            _const_spec((D_MODEL, D_MODEL)),
            _const_spec((1, D_MODEL)),
            _const_spec((D_MODEL, C_GRP)),
            _const_spec((N_MEM, C_GRP)),
            _const_spec((N_MEM, C_GRP)),
            _const_spec((C_GRP, D_MODEL)),
        ],
        out_specs=pl.BlockSpec((tm, D_MODEL), row),
        compiler_params=_params("arbitrary"),
        name="mix_out_xattn",
    )(x, yabd, yc, w_out, g, wq, km, vm, wxo)


def kernel(x, mem, ffn1_norm, ffn1_w_gate, ffn1_w_up, ffn1_w_down, mix_norm, w_mix_in, pool_w, pool_scale, conv_dw_w, conv_dw_b, conv_ln_g, conv_ln_b, conv_pw, fox_f_bias, sc_w, grp_norm, w_mix_out, xa_norm, mem_norm, w_xq, w_xk, w_xv, w_xo, ffn2_norm, ffn2_w_gate, ffn2_w_up, ffn2_w_down, final_norm):
    xs = x.reshape(SEQ, D_MODEL)
    mem2 = mem.reshape(N_MEM, D_MODEL)
    bf = lambda a: a.astype(BF16)
    vec = lambda a: a.reshape(1, -1)
    f_off = 6 * C_GRP
    for l in range(DEPTH):
        xs = _ffn(xs, vec(ffn1_norm[l]), _round_bf16(ffn1_w_gate, l), _round_bf16(ffn1_w_up, l),
                  _round_bf16(ffn1_w_down, l))

        w_in = w_mix_in[l]
        w_main = bf(jnp.concatenate([w_in[:, :f_off], w_in[:, f_off + HEADS:]], axis=1))
        w_f = bf(jnp.pad(w_in[:, f_off:f_off + HEADS], ((0, 0), (0, LANES - HEADS))))
        f_bias = jnp.pad(fox_f_bias[l], (0, LANES - HEADS)).reshape(1, LANES)
        yabd, qa, ka, vt = _mix_in(
            xs, vec(mix_norm[l]), w_main, w_f, f_bias, bf(pool_w[l]), vec(pool_scale[l]),
            conv_dw_w[l], vec(conv_dw_b[l]), vec(conv_ln_g[l]), vec(conv_ln_b[l]),
            bf(conv_pw[l]), sc_w[l], vec(grp_norm[l]))
        yc = _fox(qa, ka, vt, grp_norm[l, 2 * C_GRP:3 * C_GRP].reshape(C_GRP, 1))

        km, vm = _mem_kv(mem2, vec(mem_norm[l]), bf(w_xk[l]), bf(w_xv[l]))
        xs = _mix_out(xs, yabd, yc, bf(w_mix_out[l]), vec(xa_norm[l]), bf(w_xq[l]), km, vm, bf(w_xo[l]))

        gf = vec(final_norm) if l == DEPTH - 1 else None
        xs = _ffn(xs, vec(ffn2_norm[l]), _round_bf16(ffn2_w_gate, l), _round_bf16(ffn2_w_up, l),
                  _round_bf16(ffn2_w_down, l), gf)
    return xs.reshape(1, SEQ, D_MODEL)
```

```python
import functools

import jax
import jax.numpy as jnp
from jax import lax
from jax.experimental import pallas as pl
from jax.experimental.pallas import tpu as pltpu

F32 = jnp.float32
BF16 = jnp.bfloat16

D_MODEL = 2048
SEQ = 16384
DEPTH = 2
C_GRP = 512
POOL_WINDOWS = (2, 4, 8, 16)
POOL_GROUP = 128
CONV_K = 31
SC_K = 3
HEADS = 4
HEAD_DIM = 128
N_MEM = 256
D_FF = 5632
EPS = 1e-6
ATTN_SCALE = HEAD_DIM ** -0.5
LOG2E = 1.4426950408889634
N_SPLIT = 3
ACC_ROWS = HEAD_DIM + 16
LANES = 128
SUBLANES = 8
HALO = 32
NEG_BIG = -1e30
VMEM_LIMIT = 56 * 1024 * 1024

CAST_BLOCK_BYTES = 6 * 1024 * 1024
FFN_TM, FFN_TF = 1024, 512
MIX_TM = 256
ATT_TQ, ATT_TK = 1024, 1024
OUT_TM = 256


def _rms(x, g):
    ms = jnp.mean(x * x, axis=-1, keepdims=True)
    return x * lax.rsqrt(ms + EPS) * g


def _dot(a, b):
    return jnp.dot(a, b, preferred_element_type=F32)


def _dot_nt(a, b):
    return lax.dot_general(a, b, (((1,), (1,)), ((), ())), preferred_element_type=F32)


def _const_spec(shape):
    return pl.BlockSpec(shape, lambda *_: (0,) * len(shape), pipeline_mode=pl.Buffered(1))


def _params(*sem):
    return pltpu.CompilerParams(dimension_semantics=sem, vmem_limit_bytes=VMEM_LIMIT)


def _ffn_kernel(*refs, final):
    if final:
        x_ref, g_ref, wg_ref, wu_ref, wd_ref, gf_ref, o_ref, h_ref = refs
    else:
        x_ref, g_ref, wg_ref, wu_ref, wd_ref, o_ref, h_ref = refs
    j = pl.program_id(1)

    @pl.when(j == 0)
    def _():
        x = x_ref[...]
        h_ref[...] = _rms(x, g_ref[...]).astype(BF16)
        o_ref[...] = x

    h = h_ref[...]
    gate = _dot(h, wg_ref[...])
    up = _dot(h, wu_ref[...])
    a = (0.5 * gate * jax.nn.sigmoid(gate) * up).astype(BF16)
    o_ref[...] += _dot(a, wd_ref[...])

    if final:
        @pl.when(j == pl.num_programs(1) - 1)
        def _():
            o_ref[...] = _rms(o_ref[...], gf_ref[...])


def _ffn(x, g, wg, wu, wd, gf=None):
    final = gf is not None
    tm, tf = FFN_TM, FFN_TF
    in_specs = [
        pl.BlockSpec((tm, D_MODEL), lambda i, j: (i, 0)),
        _const_spec((1, D_MODEL)),
        pl.BlockSpec((D_MODEL, tf), lambda i, j: (0, j)),
        pl.BlockSpec((D_MODEL, tf), lambda i, j: (0, j)),
        pl.BlockSpec((tf, D_MODEL), lambda i, j: (j, 0)),
    ]
    args = [x, g, wg, wu, wd]
    if final:
        in_specs.append(_const_spec((1, D_MODEL)))
        args.append(gf)
    return pl.pallas_call(
        functools.partial(_ffn_kernel, final=final),
        out_shape=jax.ShapeDtypeStruct((SEQ, D_MODEL), F32),
        grid=(SEQ // tm, D_FF // tf),
        in_specs=in_specs,
        out_specs=pl.BlockSpec((tm, D_MODEL), lambda i, j: (i, 0)),
        scratch_shapes=[pltpu.VMEM((tm, D_MODEL), BF16)],
        compiler_params=_params("arbitrary", "arbitrary"),
        name="ffn_final" if final else "ffn",
    )(*args)


def _round_kernel(w_ref, o_ref):
    o_ref[...] = w_ref[...].astype(BF16)


def _round_bf16(w, layer):
    _, k, n = w.shape
    bk = max(b for b in range(16, k + 1, 16) if k % b == 0 and b * n * 4 <= CAST_BLOCK_BYTES)
    return pl.pallas_call(
        _round_kernel,
        out_shape=jax.ShapeDtypeStruct((k, n), BF16),
        grid=(k // bk,),
        in_specs=[pl.BlockSpec((None, bk, n), lambda i: (layer, i, 0))],
        out_specs=pl.BlockSpec((bk, n), lambda i: (i, 0)),
        compiler_params=_params("arbitrary"),
        name="round_bf16",
    )(w)


def _mem_kv_kernel(mem_ref, g_ref, wk_ref, wv_ref, k_ref, v_ref):
    mn = _rms(mem_ref[...], g_ref[...]).astype(BF16)
    k_ref[...] = _dot(mn, wk_ref[...]).astype(BF16)
    v_ref[...] = _dot(mn, wv_ref[...]).astype(BF16)


def _mem_kv(mem, g, wk, wv):
    return pl.pallas_call(
        _mem_kv_kernel,
        out_shape=[jax.ShapeDtypeStruct((N_MEM, C_GRP), BF16)] * 2,
        compiler_params=pltpu.CompilerParams(vmem_limit_bytes=VMEM_LIMIT),
        name="mem_kv",
    )(mem, g, wk, wv)


def _mix_in_kernel(x_ref, g_ref, w_ref, wf_ref, fb_ref, poolw_ref, pools_ref,
                   dww_ref, dwb_ref, lng_ref, lnb_ref, pw_ref, scw_ref, gn_ref,
                   y_ref, qa_ref, ka_ref, vt_ref,
                   pbuf, cbuf, sbuf, abuf, carry_ref):
    i = pl.program_id(0)
    tm = x_ref.shape[0]

    @pl.when(i == 0)
    def _():
        pbuf[0:HALO, :] = jnp.zeros((HALO, C_GRP), F32)
        cbuf[0:HALO, :] = jnp.zeros((HALO, C_GRP), F32)
        sbuf[0:HALO, :] = jnp.zeros((HALO, C_GRP), F32)
        carry_ref[...] = jnp.zeros_like(carry_ref)

    h = _rms(x_ref[...], g_ref[...]).astype(BF16)

    def proj(s):
        return _dot(h, w_ref[:, s * C_GRP:(s + 1) * C_GRP])

    u = proj(0)
    glu = proj(1) * jax.nn.sigmoid(proj(2))
    pbuf[HALO:HALO + tm, :] = u
    cbuf[HALO:HALO + tm, :] = glu

    pos = i * tm + lax.broadcasted_iota(jnp.int32, (tm, 1), 0)
    pooled = []
    for gi, w in enumerate(POOL_WINDOWS):
        sl = slice(gi * POOL_GROUP, (gi + 1) * POOL_GROUP)
        tot = u[:, sl]
        for d in range(1, w):
            tot = tot + pbuf[HALO - d:HALO - d + tm, sl]
        count = jnp.minimum(pos + 1, w).astype(F32)
        pooled.append((tot / count - u[:, sl]).astype(BF16))
    pbuf[0:HALO, :] = pbuf[tm:tm + HALO, :]

    first_off = HALO - (CONV_K - 1)

    def conv_taps(r, acc):
        offs = [o for o in range(first_off, first_off + CONV_K) if o % SUBLANES == r]
        base = offs[0]
        if r == 0:
            rows, shift = cbuf, base
        else:
            rows, shift = abuf.at[r - 1], 0
            n = offs[-1] + tm - base
            rows[0:n, :] = cbuf[base:base + n, :]
        for o in offs:
            lo = o - base + shift
            term = dww_ref[o - first_off:o - first_off + 1, :] * rows[lo:lo + tm, :]
            acc = term if acc is None else acc + term
        return acc

    def put_heads(ref, val):
        for hd in range(HEADS):
            ref[:, hd * 2 * HEAD_DIM:hd * 2 * HEAD_DIM + HEAD_DIM] = (
                val[:, hd * HEAD_DIM:(hd + 1) * HEAD_DIM].astype(BF16))

    conv = conv_taps(0, None)
    put_heads(qa_ref, proj(3) * (ATTN_SCALE * LOG2E))
    conv = conv_taps(1, conv)
    put_heads(ka_ref, proj(4))
    conv = conv_taps(2, conv)
    vt_ref[...] = proj(5).T.astype(BF16)
    conv = conv_taps(3, conv)
    f = _dot(h, wf_ref[...]) + fb_ref[...]
    sbuf[HALO:HALO + tm, :] = proj(7) * proj(8)
    conv = conv_taps(4, conv)

    sconv = None
    for kk in range(SC_K):
        off = HALO - (SC_K - 1) + kk
        term = scw_ref[kk:kk + 1, :] * sbuf[off:off + tm, :]
        sconv = term if sconv is None else sconv + term
    yd = proj(6) * sconv
    y_ref[:, 2 * C_GRP:3 * C_GRP] = _rms(yd, gn_ref[:, 3 * C_GRP:4 * C_GRP]).astype(BF16)
    sbuf[0:HALO, :] = sbuf[tm:tm + HALO, :]

    conv = conv_taps(5, conv)
    conv = conv_taps(6, conv)
    conv = conv_taps(7, conv)
    cbuf[0:HALO, :] = cbuf[tm:tm + HALO, :]
    conv = conv + dwb_ref[...]
    mu = jnp.mean(conv, axis=-1, keepdims=True)
    xc = conv - mu
    var = jnp.mean(xc * xc, axis=-1, keepdims=True)
    ln = xc * lax.rsqrt(var + EPS) * lng_ref[...] + lnb_ref[...]
    act = (ln * jax.nn.sigmoid(ln)).astype(BF16)
    yb = _dot(act, pw_ref[...])
    ya = jnp.concatenate([_dot(p, poolw_ref[gi]) for gi, p in enumerate(pooled)], axis=-1) * pools_ref[...]
    y_ref[:, 0:C_GRP] = _rms(ya, gn_ref[:, 0:C_GRP]).astype(BF16)
    y_ref[:, C_GRP:2 * C_GRP] = _rms(yb, gn_ref[:, C_GRP:2 * C_GRP]).astype(BF16)

    logf = jnp.minimum(f, 0.0) - jnp.log(1.0 + jnp.exp(-jnp.abs(f)))
    tri = (lax.broadcasted_iota(jnp.int32, (tm, tm), 0)
           >= lax.broadcasted_iota(jnp.int32, (tm, tm), 1)).astype(F32)
    c = jnp.dot(tri, logf, precision=lax.Precision.HIGHEST,
                preferred_element_type=F32) + carry_ref[...]
    carry_ref[...] = c[tm - 1:tm, :]
    c2 = c * LOG2E
    lane = lax.broadcasted_iota(jnp.int32, (tm, LANES), 1)
    ones = jnp.where(lane < 2 * N_SPLIT, 1.0, 0.0)
    for hd in range(HEADS):
        ch = c2[:, hd:hd + 1]
        p1 = ch.astype(BF16).astype(F32)
        r1 = ch - p1
        p2 = r1.astype(BF16).astype(F32)
        p3 = r1 - p2
        qx = jnp.where(lane == 0, p1, jnp.where(lane == 1, p2, jnp.where(lane == 2, p3, ones)))
        kx = jnp.where(lane == 3, -p1, jnp.where(lane == 4, -p2, jnp.where(lane == 5, -p3, ones)))
        base = hd * 2 * HEAD_DIM
        qa_ref[:, base + HEAD_DIM:base + 2 * HEAD_DIM] = qx.astype(BF16)
        ka_ref[:, base + HEAD_DIM:base + 2 * HEAD_DIM] = kx.astype(BF16)


def _mix_in(x, g, w_main, w_f, f_bias, pool_w, pool_scale, dw_w, dw_b, ln_g, ln_b, pw, sc_w, gn):
    tm = MIX_TM
    row = lambda i: (i, 0)
    n_sec = w_main.shape[1] // C_GRP
    out_shape = [
        jax.ShapeDtypeStruct((SEQ, 3 * C_GRP), BF16),
        jax.ShapeDtypeStruct((SEQ, 2 * C_GRP), BF16),
        jax.ShapeDtypeStruct((SEQ, 2 * C_GRP), BF16),
        jax.ShapeDtypeStruct((C_GRP, SEQ), BF16),
    ]
    out_specs = [
        pl.BlockSpec((tm, 3 * C_GRP), row),
        pl.BlockSpec((tm, 2 * C_GRP), row),
        pl.BlockSpec((tm, 2 * C_GRP), row),
        pl.BlockSpec((C_GRP, tm), lambda i: (0, i)),
    ]
    in_specs = [
        pl.BlockSpec((tm, D_MODEL), row),
        _const_spec((1, D_MODEL)),
        _const_spec((D_MODEL, n_sec * C_GRP)),
        _const_spec((D_MODEL, LANES)),
        _const_spec((1, LANES)),
        _const_spec((len(POOL_WINDOWS), POOL_GROUP, POOL_GROUP)),
        _const_spec((1, C_GRP)),
        _const_spec((CONV_K, C_GRP)),
        _const_spec((1, C_GRP)),
        _const_spec((1, C_GRP)),
        _const_spec((1, C_GRP)),
        _const_spec((C_GRP, C_GRP)),
        _const_spec((SC_K, C_GRP)),
        _const_spec((1, 4 * C_GRP)),
    ]
    return pl.pallas_call(
        _mix_in_kernel,
        out_shape=out_shape,
        grid=(SEQ // tm,),
        in_specs=in_specs,
        out_specs=out_specs,
        scratch_shapes=[pltpu.VMEM((HALO + tm, C_GRP), F32)] * 3 + [
            pltpu.VMEM((SUBLANES - 1, HALO + tm, C_GRP), F32),
            pltpu.VMEM((1, LANES), F32)],
        compiler_params=_params("arbitrary"),
        name="mix_in",
    )(x, g, w_main, w_f, f_bias, pool_w, pool_scale, dw_w, dw_b, ln_g, ln_b, pw, sc_w, gn)


def _fox_kernel(qi_ref, ki_ref, qa_ref, ka_ref, vt_ref, g_ref, o_ref, m_ref, acc_ref):
    step = pl.program_id(0)
    qi = qi_ref[step]
    ki = ki_ref[step]
    tq = qa_ref.shape[0]
    tk = ka_ref.shape[0]
    first_diag = qi * (tq // tk)
    last = first_diag + tq // tk - 1

    @pl.when(ki == 0)
    def _():
        m_ref[...] = jnp.full_like(m_ref, NEG_BIG)
        acc_ref[...] = jnp.zeros_like(acc_ref)

    def accumulate(diagonal):
        ones_rows = (lax.broadcasted_iota(jnp.int32, (ACC_ROWS - HEAD_DIM, tk), 0) == 0).astype(BF16)
        if diagonal:
            keep = (lax.broadcasted_iota(jnp.int32, (tk, tq), 0) + (ki - first_diag) * tk
                    <= lax.broadcasted_iota(jnp.int32, (tk, tq), 1))
        sts, pts, alphas = {}, {}, {}

        def scores(hd):
            wide = slice(hd * 2 * HEAD_DIM, (hd + 1) * 2 * HEAD_DIM)
            st = _dot_nt(ka_ref[:, wide], qa_ref[:, wide])
            sts[hd] = jnp.where(keep, st, NEG_BIG) if diagonal else st

        def softmax(hd):
            m_old = m_ref[hd]
            m_new = jnp.maximum(m_old, jnp.max(sts[hd], axis=0, keepdims=True))
            alphas[hd] = jnp.exp2(m_old - m_new)
            pts[hd] = jnp.exp2(sts[hd] - m_new).astype(BF16)
            m_ref[hd] = m_new

        def values(hd):
            vat = jnp.concatenate([vt_ref[hd * HEAD_DIM:(hd + 1) * HEAD_DIM, :], ones_rows], axis=0)
            acc_ref[hd] = alphas[hd] * acc_ref[hd] + _dot(vat, pts[hd])

        for hd in range(HEADS + 2):
            if hd < HEADS:
                scores(hd)
            if 0 <= hd - 1 < HEADS:
                softmax(hd - 1)
            if 0 <= hd - 2 < HEADS:
                values(hd - 2)

    @pl.when(ki < first_diag)
    def _():
        accumulate(diagonal=False)

    @pl.when(ki >= first_diag)
    def _():
        accumulate(diagonal=True)

    @pl.when(ki == last)
    def _():
        ot = jnp.concatenate(
            [acc_ref[hd, 0:HEAD_DIM, :] / acc_ref[hd, HEAD_DIM:HEAD_DIM + 1, :] for hd in range(HEADS)],
            axis=0)
        ms = jnp.mean(ot * ot, axis=0, keepdims=True)
        o_ref[...] = (ot * lax.rsqrt(ms + EPS) * g_ref[...]).T.astype(BF16)


def _fox(qa, ka, vt, g):
    tq, tk = ATT_TQ, ATT_TK
    ratio = tq // tk
    pairs = [(i, j) for i in range(SEQ // tq) for j in range((i + 1) * ratio)]
    qi = jnp.array([p[0] for p in pairs], jnp.int32)
    ki = jnp.array([p[1] for p in pairs], jnp.int32)
    qmap = lambda s, qi, ki: (qi[s], 0)
    kmap = lambda s, qi, ki: (ki[s], 0)
    return pl.pallas_call(
        _fox_kernel,
        out_shape=jax.ShapeDtypeStruct((SEQ, C_GRP), BF16),
        grid_spec=pltpu.PrefetchScalarGridSpec(
            num_scalar_prefetch=2,
            grid=(len(pairs),),
            in_specs=[
                pl.BlockSpec((tq, 2 * C_GRP), qmap),
                pl.BlockSpec((tk, 2 * C_GRP), kmap),
                pl.BlockSpec((C_GRP, tk), lambda s, qi, ki: (0, ki[s])),
                pl.BlockSpec((C_GRP, 1), lambda s, qi, ki: (0, 0)),
            ],
            out_specs=pl.BlockSpec((tq, C_GRP), qmap),
            scratch_shapes=[
                pltpu.VMEM((HEADS, 1, tq), F32),
                pltpu.VMEM((HEADS, ACC_ROWS, tq), F32),
            ],
        ),
        compiler_params=_params("arbitrary"),
        name="fox_attn",
    )(qi, ki, qa, ka, vt, g)


def _mix_out_kernel(x_ref, yabd_ref, yc_ref, wo_ref, g_ref, wq_ref, km_ref, vm_ref, wxo_ref, o_ref):
    y = jnp.concatenate([yabd_ref[:, 0:2 * C_GRP], yc_ref[...], yabd_ref[:, 2 * C_GRP:3 * C_GRP]], axis=1)
    x2 = x_ref[...] + _dot(y, wo_ref[...])

    h = _rms(x2, g_ref[...]).astype(BF16)
    q = (_dot(h, wq_ref[...]) * ATTN_SCALE).astype(BF16)
    outs = []
    for hd in range(HEADS):
        sl = slice(hd * HEAD_DIM, (hd + 1) * HEAD_DIM)
        s = _dot_nt(q[:, sl], km_ref[:, sl])
        e = jnp.exp(s - jnp.max(s, axis=-1, keepdims=True))
        p = (e / jnp.sum(e, axis=-1, keepdims=True)).astype(BF16)
        outs.append(_dot(p, vm_ref[:, sl]))
    o = jnp.concatenate(outs, axis=-1).astype(BF16)
    o_ref[...] = x2 + _dot(o, wxo_ref[...])


def _mix_out(x, yabd, yc, w_out, g, wq, km, vm, wxo):
    tm = OUT_TM
    row = lambda i: (i, 0)
    return pl.pallas_call(
        _mix_out_kernel,
        out_shape=jax.ShapeDtypeStruct((SEQ, D_MODEL), F32),
        grid=(SEQ // tm,),
        in_specs=[
            pl.BlockSpec((tm, D_MODEL), row),
            pl.BlockSpec((tm, 3 * C_GRP), row),
            pl.BlockSpec((tm, C_GRP), row),
            _const_spec((D_MODEL, D_MODEL)),
            _const_spec((1, D_MODEL)),
            _const_spec((D_MODEL, C_GRP)),
            _const_spec((N_MEM, C_GRP)),
            _const_spec((N_MEM, C_GRP)),
            _const_spec((C_GRP, D_MODEL)),
        ],
        out_specs=pl.BlockSpec((tm, D_MODEL), row),
        compiler_params=_params("arbitrary"),
        name="mix_out_xattn",
    )(x, yabd, yc, w_out, g, wq, km, vm, wxo)


def kernel(x, mem, ffn1_norm, ffn1_w_gate, ffn1_w_up, ffn1_w_down, mix_norm, w_mix_in, pool_w, pool_scale, conv_dw_w, conv_dw_b, conv_ln_g, conv_ln_b, conv_pw, fox_f_bias, sc_w, grp_norm, w_mix_out, xa_norm, mem_norm, w_xq, w_xk, w_xv, w_xo, ffn2_norm, ffn2_w_gate, ffn2_w_up, ffn2_w_down, final_norm):
    xs = x.reshape(SEQ, D_MODEL)
    mem2 = mem.reshape(N_MEM, D_MODEL)
    bf = lambda a: a.astype(BF16)
    vec = lambda a: a.reshape(1, -1)
    f_off = 6 * C_GRP
    for l in range(DEPTH):
        xs = _ffn(xs, vec(ffn1_norm[l]), _round_bf16(ffn1_w_gate, l), _round_bf16(ffn1_w_up, l),
                  _round_bf16(ffn1_w_down, l))

        w_in = w_mix_in[l]
        w_main = bf(jnp.concatenate([w_in[:, :f_off], w_in[:, f_off + HEADS:]], axis=1))
        w_f = bf(jnp.pad(w_in[:, f_off:f_off + HEADS], ((0, 0), (0, LANES - HEADS))))
        f_bias = jnp.pad(fox_f_bias[l], (0, LANES - HEADS)).reshape(1, LANES)
        yabd, qa, ka, vt = _mix_in(
            xs, vec(mix_norm[l]), w_main, w_f, f_bias, bf(pool_w[l]), vec(pool_scale[l]),
            conv_dw_w[l], vec(conv_dw_b[l]), vec(conv_ln_g[l]), vec(conv_ln_b[l]),
            bf(conv_pw[l]), sc_w[l], vec(grp_norm[l]))
        yc = _fox(qa, ka, vt, grp_norm[l, 2 * C_GRP:3 * C_GRP].reshape(C_GRP, 1))

        km, vm = _mem_kv(mem2, vec(mem_norm[l]), bf(w_xk[l]), bf(w_xv[l]))
        xs = _mix_out(xs, yabd, yc, bf(w_mix_out[l]), vec(xa_norm[l]), bf(w_xq[l]), km, vm, bf(w_xo[l]))

        gf = vec(final_norm) if l == DEPTH - 1 else None
        xs = _ffn(xs, vec(ffn2_norm[l]), _round_bf16(ffn2_w_gate, l), _round_bf16(ffn2_w_up, l),
                  _round_bf16(ffn2_w_down, l), gf)
    return xs.reshape(1, SEQ, D_MODEL)
```

```python
import functools

import jax
import jax.numpy as jnp
from jax import lax
from jax.experimental import pallas as pl
from jax.experimental.pallas import tpu as pltpu

F32 = jnp.float32
BF16 = jnp.bfloat16

D_MODEL = 2048
SEQ = 16384
DEPTH = 2
C_GRP = 512
POOL_WINDOWS = (2, 4, 8, 16)
POOL_GROUP = 128
CONV_K = 31
SC_K = 3
HEADS = 4
HEAD_DIM = 128
N_MEM = 256
D_FF = 5632
EPS = 1e-6
ATTN_SCALE = HEAD_DIM ** -0.5
LOG2E = 1.4426950408889634
N_SPLIT = 3
ACC_ROWS = HEAD_DIM + 16
LANES = 128
SUBLANES = 8
HALO = 32
NEG_BIG = -1e30
VMEM_LIMIT = 56 * 1024 * 1024

CAST_BLOCK_BYTES = 6 * 1024 * 1024
FFN_TM, FFN_TF = 1024, 512
MIX_TM = 256
ATT_TQ, ATT_TK = 1024, 1024
OUT_TM = 256


def _rms(x, g):
    ms = jnp.mean(x * x, axis=-1, keepdims=True)
    return x * lax.rsqrt(ms + EPS) * g


def _dot(a, b):
    return jnp.dot(a, b, preferred_element_type=F32)


def _dot_nt(a, b):
    return lax.dot_general(a, b, (((1,), (1,)), ((), ())), preferred_element_type=F32)


def _const_spec(shape):
    return pl.BlockSpec(shape, lambda *_: (0,) * len(shape), pipeline_mode=pl.Buffered(1))


def _params(*sem):
    return pltpu.CompilerParams(dimension_semantics=sem, vmem_limit_bytes=VMEM_LIMIT)


def _ffn_kernel(*refs, final):
    if final:
        x_ref, g_ref, wg_ref, wu_ref, wd_ref, gf_ref, o_ref, h_ref = refs
    else:
        x_ref, g_ref, wg_ref, wu_ref, wd_ref, o_ref, h_ref = refs
    j = pl.program_id(1)

    @pl.when(j == 0)
    def _():
        x = x_ref[...]
        h_ref[...] = _rms(x, g_ref[...]).astype(BF16)
        o_ref[...] = x

    h = h_ref[...]
    gate = _dot(h, wg_ref[...])
    up = _dot(h, wu_ref[...])
    a = (0.5 * gate * jax.nn.sigmoid(gate) * up).astype(BF16)
    o_ref[...] += _dot(a, wd_ref[...])

    if final:
        @pl.when(j == pl.num_programs(1) - 1)
        def _():
            o_ref[...] = _rms(o_ref[...], gf_ref[...])


def _ffn(x, g, wg, wu, wd, gf=None):
    final = gf is not None
    tm, tf = FFN_TM, FFN_TF
    in_specs = [
        pl.BlockSpec((tm, D_MODEL), lambda i, j: (i, 0)),
        _const_spec((1, D_MODEL)),
        pl.BlockSpec((D_MODEL, tf), lambda i, j: (0, j)),
        pl.BlockSpec((D_MODEL, tf), lambda i, j: (0, j)),
        pl.BlockSpec((tf, D_MODEL), lambda i, j: (j, 0)),
    ]
    args = [x, g, wg, wu, wd]
    if final:
        in_specs.append(_const_spec((1, D_MODEL)))
        args.append(gf)
    return pl.pallas_call(
        functools.partial(_ffn_kernel, final=final),
        out_shape=jax.ShapeDtypeStruct((SEQ, D_MODEL), F32),
        grid=(SEQ // tm, D_FF // tf),
        in_specs=in_specs,
        out_specs=pl.BlockSpec((tm, D_MODEL), lambda i, j: (i, 0)),
        scratch_shapes=[pltpu.VMEM((tm, D_MODEL), BF16)],
        compiler_params=_params("arbitrary", "arbitrary"),
        name="ffn_final" if final else "ffn",
    )(*args)


def _round_kernel(w_ref, o_ref):
    o_ref[...] = w_ref[...].astype(BF16)


def _round_bf16(w, layer):
    layers, k, n = w.shape
    bk = max(b for b in range(16, k + 1, 16) if k % b == 0 and b * n * 4 <= CAST_BLOCK_BYTES)
    steps = k // bk
    return pl.pallas_call(
        _round_kernel,
        out_shape=jax.ShapeDtypeStruct((k, n), BF16),
        grid=(steps,),
        in_specs=[pl.BlockSpec((bk, n), lambda i: (layer * steps + i, 0))],
        out_specs=pl.BlockSpec((bk, n), lambda i: (i, 0)),
        compiler_params=_params("arbitrary"),
        name="round_bf16",
    )(w.reshape(layers * k, n))


def _mem_kv_kernel(mem_ref, g_ref, wk_ref, wv_ref, k_ref, v_ref):
    mn = _rms(mem_ref[...], g_ref[...]).astype(BF16)
    k_ref[...] = _dot(mn, wk_ref[...]).astype(BF16)
    v_ref[...] = _dot(mn, wv_ref[...]).astype(BF16)


def _mem_kv(mem, g, wk, wv):
    return pl.pallas_call(
        _mem_kv_kernel,
        out_shape=[jax.ShapeDtypeStruct((N_MEM, C_GRP), BF16)] * 2,
        compiler_params=pltpu.CompilerParams(vmem_limit_bytes=VMEM_LIMIT),
        name="mem_kv",
    )(mem, g, wk, wv)


def _mix_in_kernel(x_ref, g_ref, w_ref, wf_ref, fb_ref, poolw_ref, pools_ref,
                   dww_ref, dwb_ref, lng_ref, lnb_ref, pw_ref, scw_ref, gn_ref,
                   y_ref, qa_ref, ka_ref, vt_ref,
                   pbuf, cbuf, sbuf, abuf, carry_ref):
    i = pl.program_id(0)
    tm = x_ref.shape[0]

    @pl.when(i == 0)
    def _():
        pbuf[0:HALO, :] = jnp.zeros((HALO, C_GRP), F32)
        cbuf[0:HALO, :] = jnp.zeros((HALO, C_GRP), F32)
        sbuf[0:HALO, :] = jnp.zeros((HALO, C_GRP), F32)
        carry_ref[...] = jnp.zeros_like(carry_ref)

    h = _rms(x_ref[...], g_ref[...]).astype(BF16)

    def proj(s):
        return _dot(h, w_ref[:, s * C_GRP:(s + 1) * C_GRP])

    u = proj(0)
    glu = proj(1) * jax.nn.sigmoid(proj(2))
    pbuf[HALO:HALO + tm, :] = u
    cbuf[HALO:HALO + tm, :] = glu

    pos = i * tm + lax.broadcasted_iota(jnp.int32, (tm, 1), 0)
    pooled = []
    for gi, w in enumerate(POOL_WINDOWS):
        sl = slice(gi * POOL_GROUP, (gi + 1) * POOL_GROUP)
        tot = u[:, sl]
        for d in range(1, w):
            tot = tot + pbuf[HALO - d:HALO - d + tm, sl]
        count = jnp.minimum(pos + 1, w).astype(F32)
        pooled.append((tot / count - u[:, sl]).astype(BF16))
    pbuf[0:HALO, :] = pbuf[tm:tm + HALO, :]

    first_off = HALO - (CONV_K - 1)

    def conv_taps(r, acc):
        offs = [o for o in range(first_off, first_off + CONV_K) if o % SUBLANES == r]
        base = offs[0]
        if r == 0:
            rows, shift = cbuf, base
        else:
            rows, shift = abuf.at[r - 1], 0
            n = offs[-1] + tm - base
            rows[0:n, :] = cbuf[base:base + n, :]
        for o in offs:
            lo = o - base + shift
            term = dww_ref[o - first_off:o - first_off + 1, :] * rows[lo:lo + tm, :]
            acc = term if acc is None else acc + term
        return acc

    def put_heads(ref, val):
        for hd in range(HEADS):
            ref[:, hd * 2 * HEAD_DIM:hd * 2 * HEAD_DIM + HEAD_DIM] = (
                val[:, hd * HEAD_DIM:(hd + 1) * HEAD_DIM].astype(BF16))

    conv = conv_taps(0, None)
    put_heads(qa_ref, proj(3) * (ATTN_SCALE * LOG2E))
    conv = conv_taps(1, conv)
    put_heads(ka_ref, proj(4))
    conv = conv_taps(2, conv)
    vt_ref[...] = proj(5).T.astype(BF16)
    conv = conv_taps(3, conv)
    f = _dot(h, wf_ref[...]) + fb_ref[...]
    sbuf[HALO:HALO + tm, :] = proj(7) * proj(8)
    conv = conv_taps(4, conv)

    sconv = None
    for kk in range(SC_K):
        off = HALO - (SC_K - 1) + kk
        term = scw_ref[kk:kk + 1, :] * sbuf[off:off + tm, :]
        sconv = term if sconv is None else sconv + term
    yd = proj(6) * sconv
    y_ref[:, 2 * C_GRP:3 * C_GRP] = _rms(yd, gn_ref[:, 3 * C_GRP:4 * C_GRP]).astype(BF16)
    sbuf[0:HALO, :] = sbuf[tm:tm + HALO, :]

    conv = conv_taps(5, conv)
    conv = conv_taps(6, conv)
    conv = conv_taps(7, conv)
    cbuf[0:HALO, :] = cbuf[tm:tm + HALO, :]
    conv = conv + dwb_ref[...]
    mu = jnp.mean(conv, axis=-1, keepdims=True)
    xc = conv - mu
    var = jnp.mean(xc * xc, axis=-1, keepdims=True)
    ln = xc * lax.rsqrt(var + EPS) * lng_ref[...] + lnb_ref[...]
    act = (ln * jax.nn.sigmoid(ln)).astype(BF16)
    yb = _dot(act, pw_ref[...])
    ya = jnp.concatenate([_dot(p, poolw_ref[gi]) for gi, p in enumerate(pooled)], axis=-1) * pools_ref[...]
    y_ref[:, 0:C_GRP] = _rms(ya, gn_ref[:, 0:C_GRP]).astype(BF16)
    y_ref[:, C_GRP:2 * C_GRP] = _rms(yb, gn_ref[:, C_GRP:2 * C_GRP]).astype(BF16)

    logf = jnp.minimum(f, 0.0) - jnp.log(1.0 + jnp.exp(-jnp.abs(f)))
    tri = (lax.broadcasted_iota(jnp.int32, (tm, tm), 0)
           >= lax.broadcasted_iota(jnp.int32, (tm, tm), 1)).astype(F32)
    c = jnp.dot(tri, logf, precision=lax.Precision.HIGHEST,
                preferred_element_type=F32) + carry_ref[...]
    carry_ref[...] = c[tm - 1:tm, :]
    c2 = c * LOG2E
    lane = lax.broadcasted_iota(jnp.int32, (tm, LANES), 1)
    ones = jnp.where(lane < 2 * N_SPLIT, 1.0, 0.0)
    for hd in range(HEADS):
        ch = c2[:, hd:hd + 1]
        p1 = ch.astype(BF16).astype(F32)
        r1 = ch - p1
        p2 = r1.astype(BF16).astype(F32)
        p3 = r1 - p2
        qx = jnp.where(lane == 0, p1, jnp.where(lane == 1, p2, jnp.where(lane == 2, p3, ones)))
        kx = jnp.where(lane == 3, -p1, jnp.where(lane == 4, -p2, jnp.where(lane == 5, -p3, ones)))
        base = hd * 2 * HEAD_DIM
        qa_ref[:, base + HEAD_DIM:base + 2 * HEAD_DIM] = qx.astype(BF16)
        ka_ref[:, base + HEAD_DIM:base + 2 * HEAD_DIM] = kx.astype(BF16)


def _mix_in(x, g, w_main, w_f, f_bias, pool_w, pool_scale, dw_w, dw_b, ln_g, ln_b, pw, sc_w, gn):
    tm = MIX_TM
    row = lambda i: (i, 0)
    n_sec = w_main.shape[1] // C_GRP
    out_shape = [
        jax.ShapeDtypeStruct((SEQ, 3 * C_GRP), BF16),
        jax.ShapeDtypeStruct((SEQ, 2 * C_GRP), BF16),
        jax.ShapeDtypeStruct((SEQ, 2 * C_GRP), BF16),
        jax.ShapeDtypeStruct((C_GRP, SEQ), BF16),
    ]
    out_specs = [
        pl.BlockSpec((tm, 3 * C_GRP), row),
        pl.BlockSpec((tm, 2 * C_GRP), row),
        pl.BlockSpec((tm, 2 * C_GRP), row),
        pl.BlockSpec((C_GRP, tm), lambda i: (0, i)),
    ]
    in_specs = [
        pl.BlockSpec((tm, D_MODEL), row),
        _const_spec((1, D_MODEL)),
        _const_spec((D_MODEL, n_sec * C_GRP)),
        _const_spec((D_MODEL, LANES)),
        _const_spec((1, LANES)),
        _const_spec((len(POOL_WINDOWS), POOL_GROUP, POOL_GROUP)),
        _const_spec((1, C_GRP)),
        _const_spec((CONV_K, C_GRP)),
        _const_spec((1, C_GRP)),
        _const_spec((1, C_GRP)),
        _const_spec((1, C_GRP)),
        _const_spec((C_GRP, C_GRP)),
        _const_spec((SC_K, C_GRP)),
        _const_spec((1, 4 * C_GRP)),
    ]
    return pl.pallas_call(
        _mix_in_kernel,
        out_shape=out_shape,
        grid=(SEQ // tm,),
        in_specs=in_specs,
        out_specs=out_specs,
        scratch_shapes=[pltpu.VMEM((HALO + tm, C_GRP), F32)] * 3 + [
            pltpu.VMEM((SUBLANES - 1, HALO + tm, C_GRP), F32),
            pltpu.VMEM((1, LANES), F32)],
        compiler_params=_params("arbitrary"),
        name="mix_in",
    )(x, g, w_main, w_f, f_bias, pool_w, pool_scale, dw_w, dw_b, ln_g, ln_b, pw, sc_w, gn)


def _fox_kernel(qi_ref, ki_ref, qa_ref, ka_ref, vt_ref, g_ref, o_ref, m_ref, acc_ref):
    step = pl.program_id(0)
    qi = qi_ref[step]
    ki = ki_ref[step]
    tq = qa_ref.shape[0]
    tk = ka_ref.shape[0]
    first_diag = qi * (tq // tk)
    last = first_diag + tq // tk - 1

    @pl.when(ki == 0)
    def _():
        m_ref[...] = jnp.full_like(m_ref, NEG_BIG)
        acc_ref[...] = jnp.zeros_like(acc_ref)

    def accumulate(diagonal):
        ones_rows = (lax.broadcasted_iota(jnp.int32, (ACC_ROWS - HEAD_DIM, tk), 0) == 0).astype(BF16)
        if diagonal:
            keep = (lax.broadcasted_iota(jnp.int32, (tk, tq), 0) + (ki - first_diag) * tk
                    <= lax.broadcasted_iota(jnp.int32, (tk, tq), 1))
        sts, pts, alphas = {}, {}, {}

        def scores(hd):
            wide = slice(hd * 2 * HEAD_DIM, (hd + 1) * 2 * HEAD_DIM)
            st = _dot_nt(ka_ref[:, wide], qa_ref[:, wide])
            sts[hd] = jnp.where(keep, st, NEG_BIG) if diagonal else st

        def softmax(hd):
            m_old = m_ref[hd]
            m_new = jnp.maximum(m_old, jnp.max(sts[hd], axis=0, keepdims=True))
            alphas[hd] = jnp.exp2(m_old - m_new)
            pts[hd] = jnp.exp2(sts[hd] - m_new).astype(BF16)
            m_ref[hd] = m_new

        def values(hd):
            vat = jnp.concatenate([vt_ref[hd * HEAD_DIM:(hd + 1) * HEAD_DIM, :], ones_rows], axis=0)
            acc_ref[hd] = alphas[hd] * acc_ref[hd] + _dot(vat, pts[hd])

        for hd in range(HEADS + 2):
            if hd < HEADS:
                scores(hd)
            if 0 <= hd - 1 < HEADS:
                softmax(hd - 1)
            if 0 <= hd - 2 < HEADS:
                values(hd - 2)

    @pl.when(ki < first_diag)
    def _():
        accumulate(diagonal=False)

    @pl.when(ki >= first_diag)
    def _():
        accumulate(diagonal=True)

    @pl.when(ki == last)
    def _():
        ot = jnp.concatenate(
            [acc_ref[hd, 0:HEAD_DIM, :] / acc_ref[hd, HEAD_DIM:HEAD_DIM + 1, :] for hd in range(HEADS)],
            axis=0)
        ms = jnp.mean(ot * ot, axis=0, keepdims=True)
        o_ref[...] = (ot * lax.rsqrt(ms + EPS) * g_ref[...]).T.astype(BF16)


def _fox(qa, ka, vt, g):
    tq, tk = ATT_TQ, ATT_TK
    ratio = tq // tk
    pairs = [(i, j) for i in range(SEQ // tq) for j in range((i + 1) * ratio)]
    qi = jnp.array([p[0] for p in pairs], jnp.int32)
    ki = jnp.array([p[1] for p in pairs], jnp.int32)
    qmap = lambda s, qi, ki: (qi[s], 0)
    kmap = lambda s, qi, ki: (ki[s], 0)
    return pl.pallas_call(
        _fox_kernel,
        out_shape=jax.ShapeDtypeStruct((SEQ, C_GRP), BF16),
        grid_spec=pltpu.PrefetchScalarGridSpec(
            num_scalar_prefetch=2,
            grid=(len(pairs),),
            in_specs=[
                pl.BlockSpec((tq, 2 * C_GRP), qmap),
                pl.BlockSpec((tk, 2 * C_GRP), kmap),
                pl.BlockSpec((C_GRP, tk), lambda s, qi, ki: (0, ki[s])),
                pl.BlockSpec((C_GRP, 1), lambda s, qi, ki: (0, 0)),
            ],
            out_specs=pl.BlockSpec((tq, C_GRP), qmap),
            scratch_shapes=[
                pltpu.VMEM((HEADS, 1, tq), F32),
                pltpu.VMEM((HEADS, ACC_ROWS, tq), F32),
            ],
        ),
        compiler_params=_params("arbitrary"),
        name="fox_attn",
    )(qi, ki, qa, ka, vt, g)


def _mix_out_kernel(x_ref, yabd_ref, yc_ref, wo_ref, g_ref, wq_ref, km_ref, vm_ref, wxo_ref, o_ref):
    y = jnp.concatenate([yabd_ref[:, 0:2 * C_GRP], yc_ref[...], yabd_ref[:, 2 * C_GRP:3 * C_GRP]], axis=1)
    x2 = x_ref[...] + _dot(y, wo_ref[...])

    h = _rms(x2, g_ref[...]).astype(BF16)
    q = (_dot(h, wq_ref[...]) * ATTN_SCALE).astype(BF16)
    outs = []
    for hd in range(HEADS):
        sl = slice(hd * HEAD_DIM, (hd + 1) * HEAD_DIM)
        s = _dot_nt(q[:, sl], km_ref[:, sl])
        e = jnp.exp(s - jnp.max(s, axis=-1, keepdims=True))
        p = (e / jnp.sum(e, axis=-1, keepdims=True)).astype(BF16)
        outs.append(_dot(p, vm_ref[:, sl]))
    o = jnp.concatenate(outs, axis=-1).astype(BF16)
    o_ref[...] = x2 + _dot(o, wxo_ref[...])


def _mix_out(x, yabd, yc, w_out, g, wq, km, vm, wxo):
    tm = OUT_TM
    row = lambda i: (i, 0)
    return pl.pallas_call(
        _mix_out_kernel,
        out_shape=jax.ShapeDtypeStruct((SEQ, D_MODEL), F32),
        grid=(SEQ // tm,),
        in_specs=[
            pl.BlockSpec((tm, D_MODEL), row),
            pl.BlockSpec((tm, 3 * C_GRP), row),
            pl.BlockSpec((tm, C_GRP), row),
            _const_spec((D_MODEL, D_MODEL)),
            _const_spec((1, D_MODEL)),
            _const_spec((D_MODEL, C_GRP)),
            _const_spec((N_MEM, C_GRP)),
            _const_spec((N_MEM, C_GRP)),
            _const_spec((C_GRP, D_MODEL)),
        ],
        out_specs=pl.BlockSpec((tm, D_MODEL), row),
        compiler_params=_params("arbitrary"),
        name="mix_out_xattn",
    )(x, yabd, yc, w_out, g, wq, km, vm, wxo)


def kernel(x, mem, ffn1_norm, ffn1_w_gate, ffn1_w_up, ffn1_w_down, mix_norm, w_mix_in, pool_w, pool_scale, conv_dw_w, conv_dw_b, conv_ln_g, conv_ln_b, conv_pw, fox_f_bias, sc_w, grp_norm, w_mix_out, xa_norm, mem_norm, w_xq, w_xk, w_xv, w_xo, ffn2_norm, ffn2_w_gate, ffn2_w_up, ffn2_w_down, final_norm):
    xs = x.reshape(SEQ, D_MODEL)
    mem2 = mem.reshape(N_MEM, D_MODEL)
    bf = lambda a: a.astype(BF16)
    vec = lambda a: a.reshape(1, -1)
    f_off = 6 * C_GRP
    for l in range(DEPTH):
        xs = _ffn(xs, vec(ffn1_norm[l]), _round_bf16(ffn1_w_gate, l), _round_bf16(ffn1_w_up, l),
                  _round_bf16(ffn1_w_down, l))

        w_in = w_mix_in[l]
        w_main = bf(jnp.concatenate([w_in[:, :f_off], w_in[:, f_off + HEADS:]], axis=1))
        w_f = bf(jnp.pad(w_in[:, f_off:f_off + HEADS], ((0, 0), (0, LANES - HEADS))))
        f_bias = jnp.pad(fox_f_bias[l], (0, LANES - HEADS)).reshape(1, LANES)
        yabd, qa, ka, vt = _mix_in(
            xs, vec(mix_norm[l]), w_main, w_f, f_bias, bf(pool_w[l]), vec(pool_scale[l]),
            conv_dw_w[l], vec(conv_dw_b[l]), vec(conv_ln_g[l]), vec(conv_ln_b[l]),
            bf(conv_pw[l]), sc_w[l], vec(grp_norm[l]))
        yc = _fox(qa, ka, vt, grp_norm[l, 2 * C_GRP:3 * C_GRP].reshape(C_GRP, 1))

        km, vm = _mem_kv(mem2, vec(mem_norm[l]), bf(w_xk[l]), bf(w_xv[l]))
        xs = _mix_out(xs, yabd, yc, _round_bf16(w_mix_out, l), vec(xa_norm[l]), _round_bf16(w_xq, l),
                      km, vm, _round_bf16(w_xo, l))

        gf = vec(final_norm) if l == DEPTH - 1 else None
        xs = _ffn(xs, vec(ffn2_norm[l]), _round_bf16(ffn2_w_gate, l), _round_bf16(ffn2_w_up, l),
                  _round_bf16(ffn2_w_down, l), gf)
    return xs.reshape(1, SEQ, D_MODEL)
```

```python
import functools

import jax
import jax.numpy as jnp
from jax import lax
from jax.experimental import pallas as pl
from jax.experimental.pallas import tpu as pltpu

F32 = jnp.float32
BF16 = jnp.bfloat16

D_MODEL = 2048
SEQ = 16384
DEPTH = 2
C_GRP = 512
POOL_WINDOWS = (2, 4, 8, 16)
POOL_GROUP = 128
CONV_K = 31
SC_K = 3
HEADS = 4
HEAD_DIM = 128
N_MEM = 256
D_FF = 5632
EPS = 1e-6
ATTN_SCALE = HEAD_DIM ** -0.5
LOG2E = 1.4426950408889634
N_SPLIT = 3
ACC_ROWS = HEAD_DIM + 16
LANES = 128
SUBLANES = 8
HALO = 32
NEG_BIG = -1e30
VMEM_LIMIT = 56 * 1024 * 1024

CAST_BLOCK_BYTES = 6 * 1024 * 1024
FFN_TM, FFN_TF = 1024, 512
MIX_TM = 256
ATT_TQ, ATT_TK = 1024, 1024
OUT_TM = 256


def _rms(x, g):
    ms = jnp.mean(x * x, axis=-1, keepdims=True)
    return x * lax.rsqrt(ms + EPS) * g


def _dot(a, b):
    return jnp.dot(a, b, preferred_element_type=F32)


def _dot_nt(a, b):
    return lax.dot_general(a, b, (((1,), (1,)), ((), ())), preferred_element_type=F32)


def _const_spec(shape):
    return pl.BlockSpec(shape, lambda *_: (0,) * len(shape), pipeline_mode=pl.Buffered(1))


def _params(*sem):
    return pltpu.CompilerParams(dimension_semantics=sem, vmem_limit_bytes=VMEM_LIMIT)


def _ffn_kernel(*refs, final):
    if final:
        x_ref, g_ref, wg_ref, wu_ref, wd_ref, gf_ref, o_ref, h_ref = refs
    else:
        x_ref, g_ref, wg_ref, wu_ref, wd_ref, o_ref, h_ref = refs
    j = pl.program_id(1)

    @pl.when(j == 0)
    def _():
        x = x_ref[...]
        h_ref[...] = _rms(x, g_ref[...]).astype(BF16)
        o_ref[...] = x

    h = h_ref[...]
    gate = _dot(h, wg_ref[...])
    up = _dot(h, wu_ref[...])
    a = (0.5 * gate * jax.nn.sigmoid(gate) * up).astype(BF16)
    o_ref[...] += _dot(a, wd_ref[...])

    if final:
        @pl.when(j == pl.num_programs(1) - 1)
        def _():
            o_ref[...] = _rms(o_ref[...], gf_ref[...])


def _ffn(x, g, wg, wu, wd, gf=None):
    final = gf is not None
    tm, tf = FFN_TM, FFN_TF
    in_specs = [
        pl.BlockSpec((tm, D_MODEL), lambda i, j: (i, 0)),
        _const_spec((1, D_MODEL)),
        pl.BlockSpec((D_MODEL, tf), lambda i, j: (0, j)),
        pl.BlockSpec((D_MODEL, tf), lambda i, j: (0, j)),
        pl.BlockSpec((tf, D_MODEL), lambda i, j: (j, 0)),
    ]
    args = [x, g, wg, wu, wd]
    if final:
        in_specs.append(_const_spec((1, D_MODEL)))
        args.append(gf)
    return pl.pallas_call(
        functools.partial(_ffn_kernel, final=final),
        out_shape=jax.ShapeDtypeStruct((SEQ, D_MODEL), F32),
        grid=(SEQ // tm, D_FF // tf),
        in_specs=in_specs,
        out_specs=pl.BlockSpec((tm, D_MODEL), lambda i, j: (i, 0)),
        scratch_shapes=[pltpu.VMEM((tm, D_MODEL), BF16)],
        compiler_params=_params("arbitrary", "arbitrary"),
        name="ffn_final" if final else "ffn",
    )(*args)


def _round_kernel(w_ref, o_ref):
    o_ref[...] = w_ref[...].astype(BF16)


def _round_bf16(w, layer):
    layers, k, n = w.shape
    bk = max(b for b in range(16, k + 1, 16) if k % b == 0 and b * n * 4 <= CAST_BLOCK_BYTES)
    steps = k // bk
    return pl.pallas_call(
        _round_kernel,
        out_shape=jax.ShapeDtypeStruct((k, n), BF16),
        grid=(steps,),
        in_specs=[pl.BlockSpec((bk, n), lambda i: (layer * steps + i, 0))],
        out_specs=pl.BlockSpec((bk, n), lambda i: (i, 0)),
        compiler_params=_params("arbitrary"),
        name="round_bf16",
    )(w.reshape(layers * k, n))


def _split_in_kernel(w_ref, main_ref, f_ref):
    f_off = 6 * C_GRP
    main_ref[:, 0:f_off] = w_ref[:, 0:f_off].astype(BF16)
    tail = w_ref[:, f_off:]
    main_ref[:, f_off:] = tail[:, HEADS:].astype(BF16)
    lane = lax.broadcasted_iota(jnp.int32, (w_ref.shape[0], LANES), 1)
    f_ref[...] = jnp.where(lane < HEADS, tail[:, 0:LANES], 0.0).astype(BF16)


def _split_in(w, layer):
    layers, k, n = w.shape
    bk = 256
    steps = k // bk
    return pl.pallas_call(
        _split_in_kernel,
        out_shape=[jax.ShapeDtypeStruct((k, n - HEADS), BF16), jax.ShapeDtypeStruct((k, LANES), BF16)],
        grid=(steps,),
        in_specs=[pl.BlockSpec((bk, n), lambda i: (layer * steps + i, 0))],
        out_specs=[pl.BlockSpec((bk, n - HEADS), lambda i: (i, 0)), pl.BlockSpec((bk, LANES), lambda i: (i, 0))],
        compiler_params=_params("arbitrary"),
        name="split_in",
    )(w.reshape(layers * k, n))


def _mem_kv_kernel(mem_ref, g_ref, wk_ref, wv_ref, k_ref, v_ref):
    mn = _rms(mem_ref[...], g_ref[...]).astype(BF16)
    k_ref[...] = _dot(mn, wk_ref[...]).astype(BF16)
    v_ref[...] = _dot(mn, wv_ref[...]).astype(BF16)


def _mem_kv(mem, g, wk, wv):
    return pl.pallas_call(
        _mem_kv_kernel,
        out_shape=[jax.ShapeDtypeStruct((N_MEM, C_GRP), BF16)] * 2,
        compiler_params=pltpu.CompilerParams(vmem_limit_bytes=VMEM_LIMIT),
        name="mem_kv",
    )(mem, g, wk, wv)


def _mix_in_kernel(x_ref, g_ref, w_ref, wf_ref, fb_ref, poolw_ref, pools_ref,
                   dww_ref, dwb_ref, lng_ref, lnb_ref, pw_ref, scw_ref, gn_ref,
                   y_ref, qa_ref, ka_ref, vt_ref,
                   pbuf, cbuf, sbuf, abuf, carry_ref):
    i = pl.program_id(0)
    tm = x_ref.shape[0]

    @pl.when(i == 0)
    def _():
        pbuf[0:HALO, :] = jnp.zeros((HALO, C_GRP), F32)
        cbuf[0:HALO, :] = jnp.zeros((HALO, C_GRP), F32)
        sbuf[0:HALO, :] = jnp.zeros((HALO, C_GRP), F32)
        carry_ref[...] = jnp.zeros_like(carry_ref)

    h = _rms(x_ref[...], g_ref[...]).astype(BF16)

    def proj(s):
        return _dot(h, w_ref[:, s * C_GRP:(s + 1) * C_GRP])

    u = proj(0)
    glu = proj(1) * jax.nn.sigmoid(proj(2))
    pbuf[HALO:HALO + tm, :] = u
    cbuf[HALO:HALO + tm, :] = glu

    pos = i * tm + lax.broadcasted_iota(jnp.int32, (tm, 1), 0)
    pooled = []
    for gi, w in enumerate(POOL_WINDOWS):
        sl = slice(gi * POOL_GROUP, (gi + 1) * POOL_GROUP)
        tot = u[:, sl]
        for d in range(1, w):
            tot = tot + pbuf[HALO - d:HALO - d + tm, sl]
        count = jnp.minimum(pos + 1, w).astype(F32)
        pooled.append((tot / count - u[:, sl]).astype(BF16))
    pbuf[0:HALO, :] = pbuf[tm:tm + HALO, :]

    first_off = HALO - (CONV_K - 1)

    def conv_taps(r, acc):
        offs = [o for o in range(first_off, first_off + CONV_K) if o % SUBLANES == r]
        base = offs[0]
        if r == 0:
            rows, shift = cbuf, base
        else:
            rows, shift = abuf.at[r - 1], 0
            n = offs[-1] + tm - base
            rows[0:n, :] = cbuf[base:base + n, :]
        for o in offs:
            lo = o - base + shift
            term = dww_ref[o - first_off:o - first_off + 1, :] * rows[lo:lo + tm, :]
            acc = term if acc is None else acc + term
        return acc

    def put_heads(ref, val):
        for hd in range(HEADS):
            ref[:, hd * 2 * HEAD_DIM:hd * 2 * HEAD_DIM + HEAD_DIM] = (
                val[:, hd * HEAD_DIM:(hd + 1) * HEAD_DIM].astype(BF16))

    conv = conv_taps(0, None)
    put_heads(qa_ref, proj(3) * (ATTN_SCALE * LOG2E))
    conv = conv_taps(1, conv)
    put_heads(ka_ref, proj(4))
    conv = conv_taps(2, conv)
    vt_ref[...] = proj(5).T.astype(BF16)
    conv = conv_taps(3, conv)
    f = _dot(h, wf_ref[...]) + fb_ref[...]
    sbuf[HALO:HALO + tm, :] = proj(7) * proj(8)
    conv = conv_taps(4, conv)

    sconv = None
    for kk in range(SC_K):
        off = HALO - (SC_K - 1) + kk
        term = scw_ref[kk:kk + 1, :] * sbuf[off:off + tm, :]
        sconv = term if sconv is None else sconv + term
    yd = proj(6) * sconv
    y_ref[:, 2 * C_GRP:3 * C_GRP] = _rms(yd, gn_ref[:, 3 * C_GRP:4 * C_GRP]).astype(BF16)
    sbuf[0:HALO, :] = sbuf[tm:tm + HALO, :]

    conv = conv_taps(5, conv)
    conv = conv_taps(6, conv)
    conv = conv_taps(7, conv)
    cbuf[0:HALO, :] = cbuf[tm:tm + HALO, :]
    conv = conv + dwb_ref[...]
    mu = jnp.mean(conv, axis=-1, keepdims=True)
    xc = conv - mu
    var = jnp.mean(xc * xc, axis=-1, keepdims=True)
    ln = xc * lax.rsqrt(var + EPS) * lng_ref[...] + lnb_ref[...]
    act = (ln * jax.nn.sigmoid(ln)).astype(BF16)
    yb = _dot(act, pw_ref[...])
    ya = jnp.concatenate([_dot(p, poolw_ref[gi]) for gi, p in enumerate(pooled)], axis=-1) * pools_ref[...]
    y_ref[:, 0:C_GRP] = _rms(ya, gn_ref[:, 0:C_GRP]).astype(BF16)
    y_ref[:, C_GRP:2 * C_GRP] = _rms(yb, gn_ref[:, C_GRP:2 * C_GRP]).astype(BF16)

    logf = jnp.minimum(f, 0.0) - jnp.log(1.0 + jnp.exp(-jnp.abs(f)))
    tri = (lax.broadcasted_iota(jnp.int32, (tm, tm), 0)
           >= lax.broadcasted_iota(jnp.int32, (tm, tm), 1)).astype(F32)
    c = jnp.dot(tri, logf, precision=lax.Precision.HIGHEST,
                preferred_element_type=F32) + carry_ref[...]
    carry_ref[...] = c[tm - 1:tm, :]
    c2 = c * LOG2E
    lane = lax.broadcasted_iota(jnp.int32, (tm, LANES), 1)
    ones = jnp.where(lane < 2 * N_SPLIT, 1.0, 0.0)
    for hd in range(HEADS):
        ch = c2[:, hd:hd + 1]
        p1 = ch.astype(BF16).astype(F32)
        r1 = ch - p1
        p2 = r1.astype(BF16).astype(F32)
        p3 = r1 - p2
        qx = jnp.where(lane == 0, p1, jnp.where(lane == 1, p2, jnp.where(lane == 2, p3, ones)))
        kx = jnp.where(lane == 3, -p1, jnp.where(lane == 4, -p2, jnp.where(lane == 5, -p3, ones)))
        base = hd * 2 * HEAD_DIM
        qa_ref[:, base + HEAD_DIM:base + 2 * HEAD_DIM] = qx.astype(BF16)
        ka_ref[:, base + HEAD_DIM:base + 2 * HEAD_DIM] = kx.astype(BF16)


def _mix_in(x, g, w_main, w_f, f_bias, pool_w, pool_scale, dw_w, dw_b, ln_g, ln_b, pw, sc_w, gn):
    tm = MIX_TM
    row = lambda i: (i, 0)
    n_sec = w_main.shape[1] // C_GRP
    out_shape = [
        jax.ShapeDtypeStruct((SEQ, 3 * C_GRP), BF16),
        jax.ShapeDtypeStruct((SEQ, 2 * C_GRP), BF16),
        jax.ShapeDtypeStruct((SEQ, 2 * C_GRP), BF16),
        jax.ShapeDtypeStruct((C_GRP, SEQ), BF16),
    ]
    out_specs = [
        pl.BlockSpec((tm, 3 * C_GRP), row),
        pl.BlockSpec((tm, 2 * C_GRP), row),
        pl.BlockSpec((tm, 2 * C_GRP), row),
        pl.BlockSpec((C_GRP, tm), lambda i: (0, i)),
    ]
    in_specs = [
        pl.BlockSpec((tm, D_MODEL), row),
        _const_spec((1, D_MODEL)),
        _const_spec((D_MODEL, n_sec * C_GRP)),
        _const_spec((D_MODEL, LANES)),
        _const_spec((1, LANES)),
        _const_spec((len(POOL_WINDOWS), POOL_GROUP, POOL_GROUP)),
        _const_spec((1, C_GRP)),
        _const_spec((CONV_K, C_GRP)),
        _const_spec((1, C_GRP)),
        _const_spec((1, C_GRP)),
        _const_spec((1, C_GRP)),
        _const_spec((C_GRP, C_GRP)),
        _const_spec((SC_K, C_GRP)),
        _const_spec((1, 4 * C_GRP)),
    ]
    return pl.pallas_call(
        _mix_in_kernel,
        out_shape=out_shape,
        grid=(SEQ // tm,),
        in_specs=in_specs,
        out_specs=out_specs,
        scratch_shapes=[pltpu.VMEM((HALO + tm, C_GRP), F32)] * 3 + [
            pltpu.VMEM((SUBLANES - 1, HALO + tm, C_GRP), F32),
            pltpu.VMEM((1, LANES), F32)],
        compiler_params=_params("arbitrary"),
        name="mix_in",
    )(x, g, w_main, w_f, f_bias, pool_w, pool_scale, dw_w, dw_b, ln_g, ln_b, pw, sc_w, gn)


def _fox_kernel(qi_ref, ki_ref, qa_ref, ka_ref, vt_ref, g_ref, o_ref, m_ref, acc_ref):
    step = pl.program_id(0)
    qi = qi_ref[step]
    ki = ki_ref[step]
    tq = qa_ref.shape[0]
    tk = ka_ref.shape[0]
    first_diag = qi * (tq // tk)
    last = first_diag + tq // tk - 1

    @pl.when(ki == 0)
    def _():
        m_ref[...] = jnp.full_like(m_ref, NEG_BIG)
        acc_ref[...] = jnp.zeros_like(acc_ref)

    def accumulate(diagonal):
        ones_rows = (lax.broadcasted_iota(jnp.int32, (ACC_ROWS - HEAD_DIM, tk), 0) == 0).astype(BF16)
        if diagonal:
            keep = (lax.broadcasted_iota(jnp.int32, (tk, tq), 0) + (ki - first_diag) * tk
                    <= lax.broadcasted_iota(jnp.int32, (tk, tq), 1))
        sts, pts, alphas = {}, {}, {}

        def scores(hd):
            wide = slice(hd * 2 * HEAD_DIM, (hd + 1) * 2 * HEAD_DIM)
            st = _dot_nt(ka_ref[:, wide], qa_ref[:, wide])
            sts[hd] = jnp.where(keep, st, NEG_BIG) if diagonal else st

        def softmax(hd):
            m_old = m_ref[hd]
            m_new = jnp.maximum(m_old, jnp.max(sts[hd], axis=0, keepdims=True))
            alphas[hd] = jnp.exp2(m_old - m_new)
            pts[hd] = jnp.exp2(sts[hd] - m_new).astype(BF16)
            m_ref[hd] = m_new

        def values(hd):
            vat = jnp.concatenate([vt_ref[hd * HEAD_DIM:(hd + 1) * HEAD_DIM, :], ones_rows], axis=0)
            acc_ref[hd] = alphas[hd] * acc_ref[hd] + _dot(vat, pts[hd])

        for hd in range(HEADS + 2):
            if hd < HEADS:
                scores(hd)
            if 0 <= hd - 1 < HEADS:
                softmax(hd - 1)
            if 0 <= hd - 2 < HEADS:
                values(hd - 2)

    @pl.when(ki < first_diag)
    def _():
        accumulate(diagonal=False)

    @pl.when(ki >= first_diag)
    def _():
        accumulate(diagonal=True)

    @pl.when(ki == last)
    def _():
        ot = jnp.concatenate(
            [acc_ref[hd, 0:HEAD_DIM, :] / acc_ref[hd, HEAD_DIM:HEAD_DIM + 1, :] for hd in range(HEADS)],
            axis=0)
        ms = jnp.mean(ot * ot, axis=0, keepdims=True)
        o_ref[...] = (ot * lax.rsqrt(ms + EPS) * g_ref[...]).T.astype(BF16)


def _fox(qa, ka, vt, g):
    tq, tk = ATT_TQ, ATT_TK
    ratio = tq // tk
    pairs = [(i, j) for i in range(SEQ // tq) for j in range((i + 1) * ratio)]
    qi = jnp.array([p[0] for p in pairs], jnp.int32)
    ki = jnp.array([p[1] for p in pairs], jnp.int32)
    qmap = lambda s, qi, ki: (qi[s], 0)
    kmap = lambda s, qi, ki: (ki[s], 0)
    return pl.pallas_call(
        _fox_kernel,
        out_shape=jax.ShapeDtypeStruct((SEQ, C_GRP), BF16),
        grid_spec=pltpu.PrefetchScalarGridSpec(
            num_scalar_prefetch=2,
            grid=(len(pairs),),
            in_specs=[
                pl.BlockSpec((tq, 2 * C_GRP), qmap),
                pl.BlockSpec((tk, 2 * C_GRP), kmap),
                pl.BlockSpec((C_GRP, tk), lambda s, qi, ki: (0, ki[s])),
                pl.BlockSpec((C_GRP, 1), lambda s, qi, ki: (0, 0)),
            ],
            out_specs=pl.BlockSpec((tq, C_GRP), qmap),
            scratch_shapes=[
                pltpu.VMEM((HEADS, 1, tq), F32),
                pltpu.VMEM((HEADS, ACC_ROWS, tq), F32),
            ],
        ),
        compiler_params=_params("arbitrary"),
        name="fox_attn",
    )(qi, ki, qa, ka, vt, g)


def _mix_out_kernel(x_ref, yabd_ref, yc_ref, wo_ref, g_ref, wq_ref, km_ref, vm_ref, wxo_ref, o_ref):
    y = jnp.concatenate([yabd_ref[:, 0:2 * C_GRP], yc_ref[...], yabd_ref[:, 2 * C_GRP:3 * C_GRP]], axis=1)
    x2 = x_ref[...] + _dot(y, wo_ref[...])

    h = _rms(x2, g_ref[...]).astype(BF16)
    q = (_dot(h, wq_ref[...]) * ATTN_SCALE).astype(BF16)
    outs = []
    for hd in range(HEADS):
        sl = slice(hd * HEAD_DIM, (hd + 1) * HEAD_DIM)
        s = _dot_nt(q[:, sl], km_ref[:, sl])
        e = jnp.exp(s - jnp.max(s, axis=-1, keepdims=True))
        p = (e / jnp.sum(e, axis=-1, keepdims=True)).astype(BF16)
        outs.append(_dot(p, vm_ref[:, sl]))
    o = jnp.concatenate(outs, axis=-1).astype(BF16)
    o_ref[...] = x2 + _dot(o, wxo_ref[...])


def _mix_out(x, yabd, yc, w_out, g, wq, km, vm, wxo):
    tm = OUT_TM
    row = lambda i: (i, 0)
    return pl.pallas_call(
        _mix_out_kernel,
        out_shape=jax.ShapeDtypeStruct((SEQ, D_MODEL), F32),
        grid=(SEQ // tm,),
        in_specs=[
            pl.BlockSpec((tm, D_MODEL), row),
            pl.BlockSpec((tm, 3 * C_GRP), row),
            pl.BlockSpec((tm, C_GRP), row),
            _const_spec((D_MODEL, D_MODEL)),
            _const_spec((1, D_MODEL)),
            _const_spec((D_MODEL, C_GRP)),
            _const_spec((N_MEM, C_GRP)),
            _const_spec((N_MEM, C_GRP)),
            _const_spec((C_GRP, D_MODEL)),
        ],
        out_specs=pl.BlockSpec((tm, D_MODEL), row),
        compiler_params=_params("arbitrary"),
        name="mix_out_xattn",
    )(x, yabd, yc, w_out, g, wq, km, vm, wxo)


def kernel(x, mem, ffn1_norm, ffn1_w_gate, ffn1_w_up, ffn1_w_down, mix_norm, w_mix_in, pool_w, pool_scale, conv_dw_w, conv_dw_b, conv_ln_g, conv_ln_b, conv_pw, fox_f_bias, sc_w, grp_norm, w_mix_out, xa_norm, mem_norm, w_xq, w_xk, w_xv, w_xo, ffn2_norm, ffn2_w_gate, ffn2_w_up, ffn2_w_down, final_norm):
    xs = x.reshape(SEQ, D_MODEL)
    mem2 = mem.reshape(N_MEM, D_MODEL)
    bf = lambda a: a.astype(BF16)
    vec = lambda a: a.reshape(1, -1)
    for l in range(DEPTH):
        xs = _ffn(xs, vec(ffn1_norm[l]), _round_bf16(ffn1_w_gate, l), _round_bf16(ffn1_w_up, l),
                  _round_bf16(ffn1_w_down, l))

        w_main, w_f = _split_in(w_mix_in, l)
        f_bias = jnp.pad(fox_f_bias[l], (0, LANES - HEADS)).reshape(1, LANES)
        yabd, qa, ka, vt = _mix_in(
            xs, vec(mix_norm[l]), w_main, w_f, f_bias, bf(pool_w[l]), vec(pool_scale[l]),
            conv_dw_w[l], vec(conv_dw_b[l]), vec(conv_ln_g[l]), vec(conv_ln_b[l]),
            bf(conv_pw[l]), sc_w[l], vec(grp_norm[l]))
        yc = _fox(qa, ka, vt, grp_norm[l, 2 * C_GRP:3 * C_GRP].reshape(C_GRP, 1))

        km, vm = _mem_kv(mem2, vec(mem_norm[l]), bf(w_xk[l]), bf(w_xv[l]))
        xs = _mix_out(xs, yabd, yc, _round_bf16(w_mix_out, l), vec(xa_norm[l]), _round_bf16(w_xq, l),
                      km, vm, _round_bf16(w_xo, l))

        gf = vec(final_norm) if l == DEPTH - 1 else None
        xs = _ffn(xs, vec(ffn2_norm[l]), _round_bf16(ffn2_w_gate, l), _round_bf16(ffn2_w_up, l),
                  _round_bf16(ffn2_w_down, l), gf)
    return xs.reshape(1, SEQ, D_MODEL)
```

```python
import functools

import jax
import jax.numpy as jnp
from jax import lax
from jax.experimental import pallas as pl
from jax.experimental.pallas import tpu as pltpu

F32 = jnp.float32
BF16 = jnp.bfloat16

D_MODEL = 2048
SEQ = 16384
DEPTH = 2
C_GRP = 512
POOL_WINDOWS = (2, 4, 8, 16)
POOL_GROUP = 128
CONV_K = 31
SC_K = 3
HEADS = 4
HEAD_DIM = 128
N_MEM = 256
D_FF = 5632
EPS = 1e-6
ATTN_SCALE = HEAD_DIM ** -0.5
LOG2E = 1.4426950408889634
N_SPLIT = 3
ACC_ROWS = HEAD_DIM + 16
LANES = 128
SUBLANES = 8
HALO = 32
NEG_BIG = -1e30
VMEM_LIMIT = 56 * 1024 * 1024

CAST_BLOCK_BYTES = 6 * 1024 * 1024
FFN_TM, FFN_TF = 1024, 512
MIX_TM = 256
ATT_TQ, ATT_TK = 1024, 1024
OUT_TM = 256


def _rms(x, g):
    ms = jnp.mean(x * x, axis=-1, keepdims=True)
    return x * lax.rsqrt(ms + EPS) * g


def _dot(a, b):
    return jnp.dot(a, b, preferred_element_type=F32)


def _dot_nt(a, b):
    return lax.dot_general(a, b, (((1,), (1,)), ((), ())), preferred_element_type=F32)


def _const_spec(shape):
    return pl.BlockSpec(shape, lambda *_: (0,) * len(shape), pipeline_mode=pl.Buffered(1))


def _params(*sem):
    return pltpu.CompilerParams(dimension_semantics=sem, vmem_limit_bytes=VMEM_LIMIT)


def _ffn_kernel(*refs, final):
    if final:
        x_ref, g_ref, wg_ref, wu_ref, wd_ref, gf_ref, o_ref, h_ref = refs
    else:
        x_ref, g_ref, wg_ref, wu_ref, wd_ref, o_ref, h_ref = refs
    j = pl.program_id(1)

    @pl.when(j == 0)
    def _():
        x = x_ref[...]
        h_ref[...] = _rms(x, g_ref[...]).astype(BF16)
        o_ref[...] = x

    h = h_ref[...]
    gate = _dot(h, wg_ref[...])
    up = _dot(h, wu_ref[...])
    a = (0.5 * gate * jax.nn.sigmoid(gate) * up).astype(BF16)
    o_ref[...] += _dot(a, wd_ref[...])

    if final:
        @pl.when(j == pl.num_programs(1) - 1)
        def _():
            o_ref[...] = _rms(o_ref[...], gf_ref[...])


def _ffn(x, g, wg, wu, wd, gf=None):
    final = gf is not None
    tm, tf = FFN_TM, FFN_TF
    in_specs = [
        pl.BlockSpec((tm, D_MODEL), lambda i, j: (i, 0)),
        _const_spec((1, D_MODEL)),
        pl.BlockSpec((D_MODEL, tf), lambda i, j: (0, j)),
        pl.BlockSpec((D_MODEL, tf), lambda i, j: (0, j)),
        pl.BlockSpec((tf, D_MODEL), lambda i, j: (j, 0)),
    ]
    args = [x, g, wg, wu, wd]
    if final:
        in_specs.append(_const_spec((1, D_MODEL)))
        args.append(gf)
    return pl.pallas_call(
        functools.partial(_ffn_kernel, final=final),
        out_shape=jax.ShapeDtypeStruct((SEQ, D_MODEL), F32),
        grid=(SEQ // tm, D_FF // tf),
        in_specs=in_specs,
        out_specs=pl.BlockSpec((tm, D_MODEL), lambda i, j: (i, 0)),
        scratch_shapes=[pltpu.VMEM((tm, D_MODEL), BF16)],
        compiler_params=_params("arbitrary", "arbitrary"),
        name="ffn_final" if final else "ffn",
    )(*args)


def _round_kernel(w_ref, o_ref):
    o_ref[...] = w_ref[...].astype(BF16)


def _round_bf16(w, layer):
    layers, k, n = w.shape
    bk = max(b for b in range(16, k + 1, 16) if k % b == 0 and b * n * 4 <= CAST_BLOCK_BYTES)
    steps = k // bk
    return pl.pallas_call(
        _round_kernel,
        out_shape=jax.ShapeDtypeStruct((k, n), BF16),
        grid=(steps,),
        in_specs=[pl.BlockSpec((bk, n), lambda i: (layer * steps + i, 0))],
        out_specs=pl.BlockSpec((bk, n), lambda i: (i, 0)),
        compiler_params=_params("arbitrary"),
        name="round_bf16",
    )(w.reshape(layers * k, n))


def _split_in_kernel(w_ref, main_ref, f_ref):
    f_off = 6 * C_GRP
    main_ref[:, 0:f_off] = w_ref[:, 0:f_off].astype(BF16)
    tail = w_ref[:, f_off:]
    main_ref[:, f_off:] = tail[:, HEADS:].astype(BF16)
    lane = lax.broadcasted_iota(jnp.int32, (w_ref.shape[0], LANES), 1)
    f_ref[...] = jnp.where(lane < HEADS, tail[:, 0:LANES], 0.0).astype(BF16)


def _split_in(w, layer):
    _, k, n = w.shape
    bk = 256
    return pl.pallas_call(
        _split_in_kernel,
        out_shape=[jax.ShapeDtypeStruct((k, n - HEADS), BF16), jax.ShapeDtypeStruct((k, LANES), BF16)],
        grid=(k // bk,),
        in_specs=[pl.BlockSpec((None, bk, n), lambda i: (layer, i, 0))],
        out_specs=[pl.BlockSpec((bk, n - HEADS), lambda i: (i, 0)), pl.BlockSpec((bk, LANES), lambda i: (i, 0))],
        compiler_params=_params("arbitrary"),
        name="split_in",
    )(w)


def _mem_kv_kernel(mem_ref, g_ref, wk_ref, wv_ref, k_ref, v_ref):
    mn = _rms(mem_ref[...], g_ref[...]).astype(BF16)
    k_ref[...] = _dot(mn, wk_ref[...]).astype(BF16)
    v_ref[...] = _dot(mn, wv_ref[...]).astype(BF16)


def _mem_kv(mem, g, wk, wv):
    return pl.pallas_call(
        _mem_kv_kernel,
        out_shape=[jax.ShapeDtypeStruct((N_MEM, C_GRP), BF16)] * 2,
        compiler_params=pltpu.CompilerParams(vmem_limit_bytes=VMEM_LIMIT),
        name="mem_kv",
    )(mem, g, wk, wv)


def _mix_in_kernel(x_ref, g_ref, w_ref, wf_ref, fb_ref, poolw_ref, pools_ref,
                   dww_ref, dwb_ref, lng_ref, lnb_ref, pw_ref, scw_ref, gn_ref,
                   y_ref, qa_ref, ka_ref, vt_ref,
                   pbuf, cbuf, sbuf, abuf, carry_ref):
    i = pl.program_id(0)
    tm = x_ref.shape[0]

    @pl.when(i == 0)
    def _():
        pbuf[0:HALO, :] = jnp.zeros((HALO, C_GRP), F32)
        cbuf[0:HALO, :] = jnp.zeros((HALO, C_GRP), F32)
        sbuf[0:HALO, :] = jnp.zeros((HALO, C_GRP), F32)
        carry_ref[...] = jnp.zeros_like(carry_ref)

    h = _rms(x_ref[...], g_ref[...]).astype(BF16)

    def proj(s):
        return _dot(h, w_ref[:, s * C_GRP:(s + 1) * C_GRP])

    u = proj(0)
    glu = proj(1) * jax.nn.sigmoid(proj(2))
    pbuf[HALO:HALO + tm, :] = u
    cbuf[HALO:HALO + tm, :] = glu

    pos = i * tm + lax.broadcasted_iota(jnp.int32, (tm, 1), 0)
    pooled = []
    for gi, w in enumerate(POOL_WINDOWS):
        sl = slice(gi * POOL_GROUP, (gi + 1) * POOL_GROUP)
        tot = u[:, sl]
        for d in range(1, w):
            tot = tot + pbuf[HALO - d:HALO - d + tm, sl]
        count = jnp.minimum(pos + 1, w).astype(F32)
        pooled.append((tot / count - u[:, sl]).astype(BF16))
    pbuf[0:HALO, :] = pbuf[tm:tm + HALO, :]

    first_off = HALO - (CONV_K - 1)

    def conv_taps(r, acc):
        offs = [o for o in range(first_off, first_off + CONV_K) if o % SUBLANES == r]
        base = offs[0]
        if r == 0:
            rows, shift = cbuf, base
        else:
            rows, shift = abuf.at[r - 1], 0
            n = offs[-1] + tm - base
            rows[0:n, :] = cbuf[base:base + n, :]
        for o in offs:
            lo = o - base + shift
            term = dww_ref[o - first_off:o - first_off + 1, :] * rows[lo:lo + tm, :]
            acc = term if acc is None else acc + term
        return acc

    def put_heads(ref, val):
        for hd in range(HEADS):
            ref[:, hd * 2 * HEAD_DIM:hd * 2 * HEAD_DIM + HEAD_DIM] = (
                val[:, hd * HEAD_DIM:(hd + 1) * HEAD_DIM].astype(BF16))

    conv = conv_taps(0, None)
    put_heads(qa_ref, proj(3) * (ATTN_SCALE * LOG2E))
    conv = conv_taps(1, conv)
    put_heads(ka_ref, proj(4))
    conv = conv_taps(2, conv)
    vt_ref[...] = proj(5).T.astype(BF16)
    conv = conv_taps(3, conv)
    f = _dot(h, wf_ref[...]) + fb_ref[...]
    sbuf[HALO:HALO + tm, :] = proj(7) * proj(8)
    conv = conv_taps(4, conv)

    sconv = None
    for kk in range(SC_K):
        off = HALO - (SC_K - 1) + kk
        term = scw_ref[kk:kk + 1, :] * sbuf[off:off + tm, :]
        sconv = term if sconv is None else sconv + term
    yd = proj(6) * sconv
    y_ref[:, 2 * C_GRP:3 * C_GRP] = _rms(yd, gn_ref[:, 3 * C_GRP:4 * C_GRP]).astype(BF16)
    sbuf[0:HALO, :] = sbuf[tm:tm + HALO, :]

    conv = conv_taps(5, conv)
    conv = conv_taps(6, conv)
    conv = conv_taps(7, conv)
    cbuf[0:HALO, :] = cbuf[tm:tm + HALO, :]
    conv = conv + dwb_ref[...]
    mu = jnp.mean(conv, axis=-1, keepdims=True)
    xc = conv - mu
    var = jnp.mean(xc * xc, axis=-1, keepdims=True)
    ln = xc * lax.rsqrt(var + EPS) * lng_ref[...] + lnb_ref[...]
    act = (ln * jax.nn.sigmoid(ln)).astype(BF16)
    yb = _dot(act, pw_ref[...])
    ya = jnp.concatenate([_dot(p, poolw_ref[gi]) for gi, p in enumerate(pooled)], axis=-1) * pools_ref[...]
    y_ref[:, 0:C_GRP] = _rms(ya, gn_ref[:, 0:C_GRP]).astype(BF16)
    y_ref[:, C_GRP:2 * C_GRP] = _rms(yb, gn_ref[:, C_GRP:2 * C_GRP]).astype(BF16)

    logf = jnp.minimum(f, 0.0) - jnp.log(1.0 + jnp.exp(-jnp.abs(f)))
    tri = (lax.broadcasted_iota(jnp.int32, (tm, tm), 0)
           >= lax.broadcasted_iota(jnp.int32, (tm, tm), 1)).astype(F32)
    c = jnp.dot(tri, logf, precision=lax.Precision.HIGHEST,
                preferred_element_type=F32) + carry_ref[...]
    carry_ref[...] = c[tm - 1:tm, :]
    c2 = c * LOG2E
    lane = lax.broadcasted_iota(jnp.int32, (tm, LANES), 1)
    ones = jnp.where(lane < 2 * N_SPLIT, 1.0, 0.0)
    for hd in range(HEADS):
        ch = c2[:, hd:hd + 1]
        p1 = ch.astype(BF16).astype(F32)
        r1 = ch - p1
        p2 = r1.astype(BF16).astype(F32)
        p3 = r1 - p2
        qx = jnp.where(lane == 0, p1, jnp.where(lane == 1, p2, jnp.where(lane == 2, p3, ones)))
        kx = jnp.where(lane == 3, -p1, jnp.where(lane == 4, -p2, jnp.where(lane == 5, -p3, ones)))
        base = hd * 2 * HEAD_DIM
        qa_ref[:, base + HEAD_DIM:base + 2 * HEAD_DIM] = qx.astype(BF16)
        ka_ref[:, base + HEAD_DIM:base + 2 * HEAD_DIM] = kx.astype(BF16)


def _mix_in(x, g, w_main, w_f, f_bias, pool_w, pool_scale, dw_w, dw_b, ln_g, ln_b, pw, sc_w, gn):
    tm = MIX_TM
    row = lambda i: (i, 0)
    n_sec = w_main.shape[1] // C_GRP
    out_shape = [
        jax.ShapeDtypeStruct((SEQ, 3 * C_GRP), BF16),
        jax.ShapeDtypeStruct((SEQ, 2 * C_GRP), BF16),
        jax.ShapeDtypeStruct((SEQ, 2 * C_GRP), BF16),
        jax.ShapeDtypeStruct((C_GRP, SEQ), BF16),
    ]
    out_specs = [
        pl.BlockSpec((tm, 3 * C_GRP), row),
        pl.BlockSpec((tm, 2 * C_GRP), row),
        pl.BlockSpec((tm, 2 * C_GRP), row),
        pl.BlockSpec((C_GRP, tm), lambda i: (0, i)),
    ]
    in_specs = [
        pl.BlockSpec((tm, D_MODEL), row),
        _const_spec((1, D_MODEL)),
        _const_spec((D_MODEL, n_sec * C_GRP)),
        _const_spec((D_MODEL, LANES)),
        _const_spec((1, LANES)),
        _const_spec((len(POOL_WINDOWS), POOL_GROUP, POOL_GROUP)),
        _const_spec((1, C_GRP)),
        _const_spec((CONV_K, C_GRP)),
        _const_spec((1, C_GRP)),
        _const_spec((1, C_GRP)),
        _const_spec((1, C_GRP)),
        _const_spec((C_GRP, C_GRP)),
        _const_spec((SC_K, C_GRP)),
        _const_spec((1, 4 * C_GRP)),
    ]
    return pl.pallas_call(
        _mix_in_kernel,
        out_shape=out_shape,
        grid=(SEQ // tm,),
        in_specs=in_specs,
        out_specs=out_specs,
        scratch_shapes=[pltpu.VMEM((HALO + tm, C_GRP), F32)] * 3 + [
            pltpu.VMEM((SUBLANES - 1, HALO + tm, C_GRP), F32),
            pltpu.VMEM((1, LANES), F32)],
        compiler_params=_params("arbitrary"),
        name="mix_in",
    )(x, g, w_main, w_f, f_bias, pool_w, pool_scale, dw_w, dw_b, ln_g, ln_b, pw, sc_w, gn)


def _fox_kernel(qi_ref, ki_ref, qa_ref, ka_ref, vt_ref, g_ref, o_ref, m_ref, acc_ref):
    step = pl.program_id(0)
    qi = qi_ref[step]
    ki = ki_ref[step]
    tq = qa_ref.shape[0]
    tk = ka_ref.shape[0]
    first_diag = qi * (tq // tk)
    last = first_diag + tq // tk - 1

    @pl.when(ki == 0)
    def _():
        m_ref[...] = jnp.full_like(m_ref, NEG_BIG)
        acc_ref[...] = jnp.zeros_like(acc_ref)

    def accumulate(diagonal):
        ones_rows = (lax.broadcasted_iota(jnp.int32, (ACC_ROWS - HEAD_DIM, tk), 0) == 0).astype(BF16)
        if diagonal:
            keep = (lax.broadcasted_iota(jnp.int32, (tk, tq), 0) + (ki - first_diag) * tk
                    <= lax.broadcasted_iota(jnp.int32, (tk, tq), 1))
        sts, pts, alphas = {}, {}, {}

        def scores(hd):
            wide = slice(hd * 2 * HEAD_DIM, (hd + 1) * 2 * HEAD_DIM)
            st = _dot_nt(ka_ref[:, wide], qa_ref[:, wide])
            sts[hd] = jnp.where(keep, st, NEG_BIG) if diagonal else st

        def softmax(hd):
            m_old = m_ref[hd]
            m_new = jnp.maximum(m_old, jnp.max(sts[hd], axis=0, keepdims=True))
            alphas[hd] = jnp.exp2(m_old - m_new)
            pts[hd] = jnp.exp2(sts[hd] - m_new).astype(BF16)
            m_ref[hd] = m_new

        def values(hd):
            vat = jnp.concatenate([vt_ref[hd * HEAD_DIM:(hd + 1) * HEAD_DIM, :], ones_rows], axis=0)
            acc_ref[hd] = alphas[hd] * acc_ref[hd] + _dot(vat, pts[hd])

        for hd in range(HEADS + 2):
            if hd < HEADS:
                scores(hd)
            if 0 <= hd - 1 < HEADS:
                softmax(hd - 1)
            if 0 <= hd - 2 < HEADS:
                values(hd - 2)

    @pl.when(ki < first_diag)
    def _():
        accumulate(diagonal=False)

    @pl.when(ki >= first_diag)
    def _():
        accumulate(diagonal=True)

    @pl.when(ki == last)
    def _():
        ot = jnp.concatenate(
            [acc_ref[hd, 0:HEAD_DIM, :] / acc_ref[hd, HEAD_DIM:HEAD_DIM + 1, :] for hd in range(HEADS)],
            axis=0)
        ms = jnp.mean(ot * ot, axis=0, keepdims=True)
        o_ref[...] = (ot * lax.rsqrt(ms + EPS) * g_ref[...]).T.astype(BF16)


def _fox(qa, ka, vt, g):
    tq, tk = ATT_TQ, ATT_TK
    ratio = tq // tk
    pairs = [(i, j) for i in range(SEQ // tq) for j in range((i + 1) * ratio)]
    qi = jnp.array([p[0] for p in pairs], jnp.int32)
    ki = jnp.array([p[1] for p in pairs], jnp.int32)
    qmap = lambda s, qi, ki: (qi[s], 0)
    kmap = lambda s, qi, ki: (ki[s], 0)
    return pl.pallas_call(
        _fox_kernel,
        out_shape=jax.ShapeDtypeStruct((SEQ, C_GRP), BF16),
        grid_spec=pltpu.PrefetchScalarGridSpec(
            num_scalar_prefetch=2,
            grid=(len(pairs),),
            in_specs=[
                pl.BlockSpec((tq, 2 * C_GRP), qmap),
                pl.BlockSpec((tk, 2 * C_GRP), kmap),
                pl.BlockSpec((C_GRP, tk), lambda s, qi, ki: (0, ki[s])),
                pl.BlockSpec((C_GRP, 1), lambda s, qi, ki: (0, 0)),
            ],
            out_specs=pl.BlockSpec((tq, C_GRP), qmap),
            scratch_shapes=[
                pltpu.VMEM((HEADS, 1, tq), F32),
                pltpu.VMEM((HEADS, ACC_ROWS, tq), F32),
            ],
        ),
        compiler_params=_params("arbitrary"),
        name="fox_attn",
    )(qi, ki, qa, ka, vt, g)


def _mix_out_kernel(x_ref, yabd_ref, yc_ref, wo_ref, g_ref, wq_ref, km_ref, vm_ref, wxo_ref, o_ref):
    y = jnp.concatenate([yabd_ref[:, 0:2 * C_GRP], yc_ref[...], yabd_ref[:, 2 * C_GRP:3 * C_GRP]], axis=1)
    x2 = x_ref[...] + _dot(y, wo_ref[...])

    h = _rms(x2, g_ref[...]).astype(BF16)
    q = (_dot(h, wq_ref[...]) * ATTN_SCALE).astype(BF16)
    outs = []
    for hd in range(HEADS):
        sl = slice(hd * HEAD_DIM, (hd + 1) * HEAD_DIM)
        s = _dot_nt(q[:, sl], km_ref[:, sl])
        e = jnp.exp(s - jnp.max(s, axis=-1, keepdims=True))
        p = (e / jnp.sum(e, axis=-1, keepdims=True)).astype(BF16)
        outs.append(_dot(p, vm_ref[:, sl]))
    o = jnp.concatenate(outs, axis=-1).astype(BF16)
    o_ref[...] = x2 + _dot(o, wxo_ref[...])


def _mix_out(x, yabd, yc, w_out, g, wq, km, vm, wxo):
    tm = OUT_TM
    row = lambda i: (i, 0)
    return pl.pallas_call(
        _mix_out_kernel,
        out_shape=jax.ShapeDtypeStruct((SEQ, D_MODEL), F32),
        grid=(SEQ // tm,),
        in_specs=[
            pl.BlockSpec((tm, D_MODEL), row),
            pl.BlockSpec((tm, 3 * C_GRP), row),
            pl.BlockSpec((tm, C_GRP), row),
            _const_spec((D_MODEL, D_MODEL)),
            _const_spec((1, D_MODEL)),
            _const_spec((D_MODEL, C_GRP)),
            _const_spec((N_MEM, C_GRP)),
            _const_spec((N_MEM, C_GRP)),
            _const_spec((C_GRP, D_MODEL)),
        ],
        out_specs=pl.BlockSpec((tm, D_MODEL), row),
        compiler_params=_params("arbitrary"),
        name="mix_out_xattn",
    )(x, yabd, yc, w_out, g, wq, km, vm, wxo)


def kernel(x, mem, ffn1_norm, ffn1_w_gate, ffn1_w_up, ffn1_w_down, mix_norm, w_mix_in, pool_w, pool_scale, conv_dw_w, conv_dw_b, conv_ln_g, conv_ln_b, conv_pw, fox_f_bias, sc_w, grp_norm, w_mix_out, xa_norm, mem_norm, w_xq, w_xk, w_xv, w_xo, ffn2_norm, ffn2_w_gate, ffn2_w_up, ffn2_w_down, final_norm):
    xs = x.reshape(SEQ, D_MODEL)
    mem2 = mem.reshape(N_MEM, D_MODEL)
    bf = lambda a: a.astype(BF16)
    vec = lambda a: a.reshape(1, -1)
    for l in range(DEPTH):
        xs = _ffn(xs, vec(ffn1_norm[l]), _round_bf16(ffn1_w_gate, l), _round_bf16(ffn1_w_up, l),
                  _round_bf16(ffn1_w_down, l))

        w_main, w_f = _split_in(w_mix_in, l)
        f_bias = jnp.pad(fox_f_bias[l], (0, LANES - HEADS)).reshape(1, LANES)
        yabd, qa, ka, vt = _mix_in(
            xs, vec(mix_norm[l]), w_main, w_f, f_bias, bf(pool_w[l]), vec(pool_scale[l]),
            conv_dw_w[l], vec(conv_dw_b[l]), vec(conv_ln_g[l]), vec(conv_ln_b[l]),
            bf(conv_pw[l]), sc_w[l], vec(grp_norm[l]))
        yc = _fox(qa, ka, vt, grp_norm[l, 2 * C_GRP:3 * C_GRP].reshape(C_GRP, 1))

        km, vm = _mem_kv(mem2, vec(mem_norm[l]), bf(w_xk[l]), bf(w_xv[l]))
        xs = _mix_out(xs, yabd, yc, _round_bf16(w_mix_out, l), vec(xa_norm[l]), _round_bf16(w_xq, l),
                      km, vm, _round_bf16(w_xo, l))

        gf = vec(final_norm) if l == DEPTH - 1 else None
        xs = _ffn(xs, vec(ffn2_norm[l]), _round_bf16(ffn2_w_gate, l), _round_bf16(ffn2_w_up, l),
                  _round_bf16(ffn2_w_down, l), gf)
    return xs.reshape(1, SEQ, D_MODEL)
```
